```python
import jax, jax.numpy as jnp
from jax import lax
import numpy as np

D_MODEL = 4096
BATCH = 1
SEQ = 8192
DEPTH = 1

MIX_WIDTH = D_MODEL
CONV_WIDTH = MIX_WIDTH // 2
HG_WIDTH = MIX_WIDTH - CONV_WIDTH
HG_EXPAND = 128
HG_HEADS = HG_WIDTH // HG_EXPAND
HG_HEAD_DIM = HG_WIDTH // HG_HEADS
CONV_KSIZE = 31
CONV_GROUPS = 16
CHUNK = 64
D_FF = ((8 * D_MODEL // 3 + 255) // 256) * 256
IN_COLS = 2 * CONV_WIDTH + 4 * HG_WIDTH
SPLITS = tuple(int(s) for s in np.cumsum([CONV_WIDTH, CONV_WIDTH, HG_WIDTH, HG_WIDTH, HG_WIDTH]))
EPS = 1e-6

kernel_name = "hymba_conformer_hgrn2_block"


def rmsnorm(x, g):
    xf = x.astype(jnp.float32)
    y = xf * lax.rsqrt(jnp.mean(xf * xf, axis=-1, keepdims=True) + EPS)
    return (y * g.astype(jnp.float32)).astype(x.dtype)


def layernorm(x, g, b):
    xf = x.astype(jnp.float32)
    mu = jnp.mean(xf, axis=-1, keepdims=True)
    var = jnp.mean(jnp.square(xf - mu), axis=-1, keepdims=True)
    y = (xf - mu) * lax.rsqrt(var + EPS)
    return (y * g.astype(jnp.float32) + b.astype(jnp.float32)).astype(x.dtype)


def conformer_conv(xv, xg, w_dw, b_dw, ln_g, ln_b):
    h = xv * jax.nn.sigmoid(xg)
    hp = jnp.pad(h, ((0, 0), (CONV_KSIZE - 1, 0), (0, 0)))
    y = lax.conv_general_dilated(
        hp, w_dw[:, None, :].astype(h.dtype), window_strides=(1,), padding='VALID',
        dimension_numbers=('NWC', 'WIO', 'NWC'), feature_group_count=CONV_WIDTH)
    y = y + b_dw
    y = layernorm(y, ln_g, ln_b)
    return jax.nn.silu(y)


def hgrn2_mixer(q, f_logit, i, g, lb, norm_g):
    B, S, _ = q.shape
    n_chunks = S // CHUNK
    f32 = jnp.float32
    lbf = lb.astype(f32)
    qf = jax.nn.silu(q.astype(f32))
    f = lbf + (1.0 - lbf) * jax.nn.sigmoid(f_logit.astype(f32))
    k = 1.0 - f
    log_f = jnp.log(f)

    def to_chunks(t, d):
        return t.reshape(B, n_chunks, CHUNK, HG_HEADS, d).transpose(1, 0, 3, 2, 4)

    qc = to_chunks(qf, HG_EXPAND)
    kc = to_chunks(k, HG_EXPAND)
    vc = to_chunks(i.astype(f32), HG_HEAD_DIM)
    bc = jnp.cumsum(to_chunks(log_f, HG_EXPAND), axis=3)
    causal = jnp.tril(jnp.ones((CHUNK, CHUNK), dtype=bool))[:, :, None]

    def step(s_prev, inp):
        qj, kj, vj, bj = inp
        diff = bj[:, :, :, None, :] - bj[:, :, None, :, :]
        decay = jnp.exp(jnp.where(causal, diff, -jnp.inf))
        scores = jnp.einsum('bhtsk,bhsk->bhts', qj[:, :, :, None, :] * decay, kj)
        o = (jnp.einsum('bhts,bhsv->bhtv', scores, vj)
             + jnp.einsum('bhtk,bhkv->bhtv', qj * jnp.exp(bj), s_prev))
        b_last = bj[:, :, -1:, :]
        s_new = (jnp.exp(b_last[:, :, 0, :])[..., None] * s_prev
                 + jnp.einsum('bhsk,bhsv->bhkv', kj * jnp.exp(b_last - bj), vj))
        return s_new, o

    s0 = jnp.zeros((B, HG_HEADS, HG_EXPAND, HG_HEAD_DIM), f32)
    _, oc = lax.scan(step, s0, (qc, kc, vc, bc))
    o = oc.transpose(1, 0, 3, 2, 4).reshape(B, S, HG_HEADS, HG_HEAD_DIM)
    o = rmsnorm(o, norm_g.reshape(HG_HEADS, HG_HEAD_DIM))
    o = o.reshape(B, S, HG_WIDTH) * jax.nn.silu(g.astype(f32))
    return o.astype(q.dtype)


def setup_inputs(seed: int = 0) -> dict:
    key = jax.random.key(seed)
    ks = jax.random.split(key, 16)
    f32 = jnp.float32
    nrm = lambda k, shape, scale: jax.random.normal(k, shape, f32) * scale
    return {
        "x": nrm(ks[0], (BATCH, SEQ, D_MODEL), 1.0),
        "attn_norm_g": 1.0 + nrm(ks[1], (DEPTH, D_MODEL), 0.02),
        "w_in": nrm(ks[2], (DEPTH, D_MODEL, IN_COLS), D_MODEL ** -0.5),
        "conv_w": nrm(ks[3], (DEPTH, CONV_KSIZE, CONV_WIDTH), CONV_KSIZE ** -0.5),
        "conv_b": nrm(ks[4], (DEPTH, CONV_WIDTH), 0.02),
        "conv_ln_g": 1.0 + nrm(ks[5], (DEPTH, CONV_WIDTH), 0.02),
        "conv_ln_b": nrm(ks[6], (DEPTH, CONV_WIDTH), 0.02),
        "hg_lb_logits": nrm(ks[7], (DEPTH + 1, HG_WIDTH), 0.5),
        "hg_norm_g": 1.0 + nrm(ks[8], (DEPTH, HG_WIDTH), 0.02),
        "w_out": nrm(ks[9], (DEPTH, MIX_WIDTH, D_MODEL), MIX_WIDTH ** -0.5),
        "ffn_norm_g": 1.0 + nrm(ks[10], (DEPTH, D_MODEL), 0.02),
        "w_gate": nrm(ks[11], (DEPTH, D_MODEL, D_FF), D_MODEL ** -0.5),
        "w_up": nrm(ks[12], (DEPTH, D_MODEL, D_FF), D_MODEL ** -0.5),
        "w_down": nrm(ks[13], (DEPTH, D_FF, D_MODEL), D_FF ** -0.5),
        "final_norm_g": 1.0 + nrm(ks[14], (D_MODEL,), 0.02),
    }


def reference(x, attn_norm_g, w_in, conv_w, conv_b, conv_ln_g, conv_ln_b, hg_lb_logits,
              hg_norm_g, w_out, ffn_norm_g, w_gate, w_up, w_down, final_norm_g):
    lb_all = jnp.cumsum(jax.nn.softmax(hg_lb_logits.astype(jnp.float32), axis=0), axis=0)
    for l in range(DEPTH):
        h = rmsnorm(x, attn_norm_g[l])
        proj = jnp.einsum('bsd,dc->bsc', h, w_in[l])
        xv, xg, q, f_logit, i, g = jnp.split(proj, SPLITS, axis=-1)
        a_out = conformer_conv(xv, xg, conv_w[l], conv_b[l], conv_ln_g[l], conv_ln_b[l])
        b_out = hgrn2_mixer(q, f_logit, i, g, lb_all[l], hg_norm_g[l])
        mix = jnp.concatenate([a_out, b_out], axis=-1)
        x = x + jnp.einsum('bsm,md->bsd', mix, w_out[l])
        h = rmsnorm(x, ffn_norm_g[l])
        u = jax.nn.silu(jnp.einsum('bsd,df->bsf', h, w_gate[l])) * jnp.einsum('bsd,df->bsf', h, w_up[l])
        x = x + jnp.einsum('bsf,fd->bsd', u, w_down[l])
    return rmsnorm(x, final_norm_g)
```

```python
import functools

import numpy as np
import jax
import jax.numpy as jnp
from jax import lax
from jax.experimental import pallas as pl
from jax.experimental.pallas import tpu as pltpu

F32 = jnp.float32
BF16 = jnp.bfloat16
EPS = 1e-6

LANES = 128
SUBLANES = 8
VMEM_BYTES = 64 * 1024 * 1024
VMEM_HEADROOM = 6 * 1024 * 1024

HG_DK = 128
HG_DV = 128
CHUNK = 64
N_LEVELS = 6
CONV_K = 31
CONV_HALO = 32


def _vmem_limit(block_bytes):
    return int(min(VMEM_BYTES - 2 * 1024 * 1024, 2 * block_bytes + VMEM_HEADROOM))


def _sigmoid(x):
    return 1.0 / (1.0 + jnp.exp(-x))


def _rmsnorm_kernel(x_ref, g_ref, o_ref):
    x = x_ref[...]
    ms = jnp.mean(x * x, axis=-1, keepdims=True)
    o_ref[...] = (x * lax.rsqrt(ms + EPS) * g_ref[...]).astype(o_ref.dtype)


def _rmsnorm(x, g, out_dtype, bm=256):
    m, d = x.shape
    return pl.pallas_call(
        _rmsnorm_kernel,
        grid=(m // bm,),
        in_specs=[pl.BlockSpec((bm, d), lambda i: (i, 0)),
                  pl.BlockSpec((1, d), lambda i: (0, 0))],
        out_specs=pl.BlockSpec((bm, d), lambda i: (i, 0)),
        out_shape=jax.ShapeDtypeStruct((m, d), out_dtype),
        compiler_params=pltpu.CompilerParams(
            dimension_semantics=("arbitrary",),
            vmem_limit_bytes=_vmem_limit(bm * d * 8)),
        name="rmsnorm",
    )(x, g.reshape(1, d))


def _matmul_kernel(*refs, n_lhs, has_res, nk):
    lhs = refs[:n_lhs]
    ws = refs[n_lhs:2 * n_lhs]
    res_ref = refs[2 * n_lhs] if has_res else None
    o_ref = refs[-1]
    acc = None
    for a_ref, w_ref in zip(lhs, ws):
        d = jnp.dot(a_ref[...], w_ref[...], preferred_element_type=F32)
        acc = d if acc is None else acc + d
    if nk == 1:
        if has_res:
            acc = acc + res_ref[...]
        o_ref[...] = acc.astype(o_ref.dtype)
    else:
        k = pl.program_id(2)

        @pl.when(k == 0)
        def _():
            o_ref[...] = acc + res_ref[...] if has_res else acc

        @pl.when(k > 0)
        def _():
            o_ref[...] += acc


def _matmul(lhs_list, w_list, res, out_dtype, bm, bn, bk=None):
    m, kdim = lhs_list[0].shape
    n = w_list[0].shape[1]
    bk = kdim if bk is None else bk
    nk = kdim // bk
    assert m % bm == 0 and n % bn == 0 and kdim % bk == 0
    assert nk == 1 or out_dtype == F32
    n_lhs = len(lhs_list)
    in_specs = ([pl.BlockSpec((bm, bk), lambda i, j, k: (i, k))] * n_lhs
                + [pl.BlockSpec((bk, bn), lambda i, j, k: (k, j))] * n_lhs)
    args = list(lhs_list) + list(w_list)
    blk = n_lhs * (bm * bk + bk * bn) * 2 + bm * bn * jnp.dtype(out_dtype).itemsize
    if res is not None:
        in_specs.append(pl.BlockSpec((bm, bn), lambda i, j, k: (i, j)))
        args.append(res)
        blk += bm * bn * 4
    return pl.pallas_call(
        functools.partial(_matmul_kernel, n_lhs=n_lhs, has_res=res is not None, nk=nk),
        grid=(m // bm, n // bn, nk),
        in_specs=in_specs,
        out_specs=pl.BlockSpec((bm, bn), lambda i, j, k: (i, j)),
        out_shape=jax.ShapeDtypeStruct((m, n), out_dtype),
        compiler_params=pltpu.CompilerParams(
            dimension_semantics=("arbitrary", "arbitrary", "arbitrary"),
            vmem_limit_bytes=_vmem_limit(blk + bm * bn * 4)),
        name="matmul",
    )(*args)


def _swiglu_kernel(h_ref, w_ref, u_ref, *, bf):
    a = jnp.dot(h_ref[...], w_ref[...], preferred_element_type=F32)
    g = a[:, :bf]
    u_ref[...] = (g * _sigmoid(g) * a[:, bf:]).astype(u_ref.dtype)


def _swiglu(h, w_gu, d_ff, bm, bf):
    m, d = h.shape
    blk = (bm * d + d * 2 * bf + bm * bf) * 2
    return pl.pallas_call(
        functools.partial(_swiglu_kernel, bf=bf),
        grid=(m // bm, d_ff // bf),
        in_specs=[pl.BlockSpec((bm, d), lambda i, j: (i, 0)),
                  pl.BlockSpec((d, 2 * bf), lambda i, j: (0, j))],
        out_specs=pl.BlockSpec((bm, bf), lambda i, j: (i, j)),
        out_shape=jax.ShapeDtypeStruct((m, d_ff), BF16),
        compiler_params=pltpu.CompilerParams(
            dimension_semantics=("arbitrary", "arbitrary"),
            vmem_limit_bytes=_vmem_limit(blk + bm * 2 * bf * 4)),
        name="swiglu",
    )(h, w_gu)


CONV_CW = 128
CONV_R = 64


def _conv_kernel(xv_ref, xg_ref, hv_ref, hg_ref, w_ref, b_ref, lg_ref, lb_ref, o_ref,
                 hext_ref, sh_ref, y_ref, *, t_rows, width):
    i = pl.program_id(0)
    hext_ref[CONV_HALO:, :] = xv_ref[...] * _sigmoid(xg_ref[...])
    halo = hv_ref[...] * _sigmoid(hg_ref[...])
    hext_ref[:CONV_HALO, :] = jnp.where(i > 0, halo, 0.0)

    first = CONV_HALO - (CONV_K - 1)
    n_sh = t_rows + CONV_HALO - SUBLANES
    for c in range(width // CONV_CW):
        cs = slice(c * CONV_CW, (c + 1) * CONV_CW)
        for b in range(1, SUBLANES):
            sh_ref[b - 1] = hext_ref[pl.ds(b, n_sh), cs]

        def row_tile(r, carry, cs=cs):
            r0 = pl.multiple_of(r * CONV_R, CONV_R)
            acc = jnp.zeros((CONV_R, CONV_CW), F32)
            for k in range(CONV_K):
                a, b = divmod(first + k, SUBLANES)
                if b == 0:
                    src = hext_ref[pl.ds(r0 + SUBLANES * a, CONV_R), cs]
                else:
                    src = sh_ref[b - 1, pl.ds(r0 + SUBLANES * a, CONV_R), :]
                acc = acc + w_ref[k:k + 1, cs] * src
            y_ref[pl.ds(r0, CONV_R), cs] = acc + b_ref[:, cs]
            return carry

        lax.fori_loop(0, t_rows // CONV_R, row_tile, 0)

    y = y_ref[...]
    mu = jnp.mean(y, axis=-1, keepdims=True)
    yc = y - mu
    var = jnp.mean(yc * yc, axis=-1, keepdims=True)
    z = yc * lax.rsqrt(var + EPS) * lg_ref[...] + lb_ref[...]
    o_ref[...] = (z * _sigmoid(z)).astype(o_ref.dtype)


def _conformer_conv(proj, conv_w, conv_b, ln_g, ln_b, width, t_rows=256):
    s = proj.shape[0]
    hb = t_rows // CONV_HALO
    row = lambda a: a.reshape(1, width)
    halo_map = lambda c: (lambda i: (jnp.maximum(i * hb - 1, 0), c))
    blk = (2 * t_rows + 2 * CONV_HALO + 40) * width * 4 + t_rows * width * 2
    scratch = ((t_rows + CONV_HALO) * width + 7 * (t_rows + CONV_HALO) * CONV_CW + t_rows * width) * 4
    return pl.pallas_call(
        functools.partial(_conv_kernel, t_rows=t_rows, width=width),
        grid=(s // t_rows,),
        in_specs=[pl.BlockSpec((t_rows, width), lambda i: (i, 0)),
                  pl.BlockSpec((t_rows, width), lambda i: (i, 1)),
                  pl.BlockSpec((CONV_HALO, width), halo_map(0)),
                  pl.BlockSpec((CONV_HALO, width), halo_map(1)),
                  pl.BlockSpec((CONV_K, width), lambda i: (0, 0)),
                  pl.BlockSpec((1, width), lambda i: (0, 0)),
                  pl.BlockSpec((1, width), lambda i: (0, 0)),
                  pl.BlockSpec((1, width), lambda i: (0, 0))],
        out_specs=pl.BlockSpec((t_rows, width), lambda i: (i, 0)),
        out_shape=jax.ShapeDtypeStruct((s, width), BF16),
        scratch_shapes=[pltpu.VMEM((t_rows + CONV_HALO, width), F32),
                        pltpu.VMEM((SUBLANES - 1, t_rows + CONV_HALO - SUBLANES, CONV_CW), F32),
                        pltpu.VMEM((t_rows, width), F32)],
        compiler_params=pltpu.CompilerParams(
            dimension_semantics=("arbitrary",),
            vmem_limit_bytes=_vmem_limit(blk + scratch)),
        name="conformer_conv",
    )(proj, proj, proj, proj, conv_w, row(conv_b), row(ln_g), row(ln_b))


def _decay_sum_matrix():
    c = CHUNK
    m = np.zeros((2 + N_LEVELS, c, c), np.float32)
    for t in range(c):
        m[0, t, :t + 1] = 1.0
        m[1, t, t + 1:] = 1.0
        for l in range(N_LEVELS):
            blk = c >> l
            mid = t - t % blk + blk // 2
            if t >= mid:
                m[2 + l, t, mid:t + 1] = 1.0
            else:
                m[2 + l, t, t + 1:mid] = 1.0
    return m.reshape((2 + N_LEVELS) * c, c)


def _hgrn_kernel(q_ref, f_ref, v_ref, g_ref, lbl_ref, ng_ref, m_ref, o_ref, st_ref,
                 *, tb, hb, layer):
    t_blk = pl.program_id(1)

    @pl.when(t_blk == 0)
    def _():
        st_ref[...] = jnp.zeros_like(st_ref)

    c = CHUNK
    lbl = lbl_ref[...]
    e = jnp.exp(lbl - jnp.max(lbl, axis=0, keepdims=True))
    lb_all = jnp.sum(e[:layer + 1], axis=0, keepdims=True) / jnp.sum(e, axis=0, keepdims=True)

    m_all = m_ref[...]
    row = lax.broadcasted_iota(jnp.int32, (c, 1), 0)
    col = lax.broadcasted_iota(jnp.int32, (1, c), 1)
    upper, same_blk = [], []
    for l in range(N_LEVELS):
        blk = c >> l
        upper.append((row % blk) >= (blk // 2))
        same_blk.append((row // blk) == (col // blk))

    nt_dims = (((1,), (1,)), ((), ()))
    tn_dims = (((0,), (0,)), ((), ()))

    for h in range(hb):
        hs = slice(h * HG_DK, (h + 1) * HG_DK)
        lb = lb_all[:, hs]
        ng = ng_ref[:, hs]
        st = st_ref[h]
        for ci in range(tb // c):
            rs = slice(ci * c, (ci + 1) * c)
            q = q_ref[rs, hs]
            v = v_ref[rs, hs]
            g = g_ref[rs, hs]
            qf = q * _sigmoid(q)
            f = lb + (1.0 - lb) * _sigmoid(f_ref[rs, hs])
            kk = 1.0 - f
            log_f = jnp.log(f)
            p0 = log_f.astype(BF16)
            r1 = log_f - p0.astype(F32)
            p1 = r1.astype(BF16)
            p2 = (r1 - p1.astype(F32)).astype(BF16)
            parts = jnp.dot(m_all, jnp.concatenate([p0, p1, p2], axis=1),
                            preferred_element_type=F32)
            ex = parts[:, :HG_DK] + parts[:, HG_DK:2 * HG_DK] + parts[:, 2 * HG_DK:]
            b_cum = ex[:c]
            qe = (qf * jnp.exp(b_cum)).astype(BF16)
            ke = (kk * jnp.exp(ex[c:2 * c])).astype(BF16)
            vb = v.astype(BF16)

            scores = None
            for l in range(N_LEVELS):
                x = jnp.exp(ex[(2 + l) * c:(3 + l) * c])
                qs = jnp.where(upper[l], qf * x, 0.0).astype(BF16)
                ks = jnp.where(upper[l], 0.0, kk * x).astype(BF16)
                s_l = lax.dot_general(qs, ks, nt_dims, preferred_element_type=F32)
                if l > 0:
                    s_l = jnp.where(same_blk[l], s_l, 0.0)
                scores = s_l if scores is None else scores + s_l

            o = jnp.dot(scores.astype(BF16), vb, preferred_element_type=F32)
            o = o + lax.dot_general(qe, st.astype(BF16), nt_dims, preferred_element_type=F32)
            o = o + jnp.sum(qf * kk, axis=-1, keepdims=True) * v

            ut = lax.dot_general(vb, ke, tn_dims, preferred_element_type=F32)
            st = st * jnp.exp(b_cum[c - 1:c, :]) + ut

            ms = jnp.mean(o * o, axis=-1, keepdims=True)
            on = o * lax.rsqrt(ms + EPS) * ng
            o_ref[rs, hs] = (on * (g * _sigmoid(g))).astype(o_ref.dtype)
        st_ref[h] = st


def _hgrn2(proj, lb_logits, norm_g, layer, col0, width, tb=512, hb=2):
    s = proj.shape[0]
    n_heads = width // HG_DK
    bw = hb * HG_DK
    sec = lambda k: (lambda hg, t: (t, (col0 + k * width) // bw + hg))
    depth1 = lb_logits.shape[0]
    m_all = jnp.asarray(_decay_sum_matrix(), BF16)
    blk = 4 * tb * bw * 4 + tb * bw * 2 + m_all.size * 2
    return pl.pallas_call(
        functools.partial(_hgrn_kernel, tb=tb, hb=hb, layer=layer),
        grid=(n_heads // hb, s // tb),
        in_specs=[pl.BlockSpec((tb, bw), sec(0)),
                  pl.BlockSpec((tb, bw), sec(1)),
                  pl.BlockSpec((tb, bw), sec(2)),
                  pl.BlockSpec((tb, bw), sec(3)),
                  pl.BlockSpec((depth1, bw), lambda hg, t: (0, hg)),
                  pl.BlockSpec((1, bw), lambda hg, t: (0, hg)),
                  pl.BlockSpec(m_all.shape, lambda hg, t: (0, 0))],
        out_specs=pl.BlockSpec((tb, bw), lambda hg, t: (t, hg)),
        out_shape=jax.ShapeDtypeStruct((s, width), BF16),
        scratch_shapes=[pltpu.VMEM((hb, HG_DV, HG_DK), F32)],
        compiler_params=pltpu.CompilerParams(
            dimension_semantics=("arbitrary", "arbitrary"),
            vmem_limit_bytes=_vmem_limit(blk + 8 * 1024 * 1024)),
        name="hgrn2",
    )(proj, proj, proj, proj, lb_logits, norm_g.reshape(1, width), m_all)


def kernel(x, attn_norm_g, w_in, conv_w, conv_b, conv_ln_g, conv_ln_b, hg_lb_logits, hg_norm_g,
           w_out, ffn_norm_g, w_gate, w_up, w_down, final_norm_g):
    bsz, seq, d_model = x.shape
    assert bsz == 1
    depth = w_in.shape[0]
    conv_width = conv_w.shape[2]
    hg_width = hg_norm_g.shape[1]
    d_ff = w_gate.shape[2]
    bf = 256
    assert d_ff % bf == 0

    xs = x.reshape(seq, d_model)
    for l in range(depth):
        w_in_b = w_in[l].astype(BF16)
        w_out_b = w_out[l].astype(BF16)
        w_down_b = w_down[l].astype(BF16)
        w_gu = jnp.stack([w_gate[l].reshape(d_model, d_ff // bf, bf),
                          w_up[l].reshape(d_model, d_ff // bf, bf)], axis=2)
        w_gu = w_gu.reshape(d_model, 2 * d_ff).astype(BF16)

        h = _rmsnorm(xs, attn_norm_g[l], BF16)
        proj = _matmul([h], [w_in_b], None, F32, bm=1024, bn=1024)
        a_out = _conformer_conv(proj, conv_w[l], conv_b[l], conv_ln_g[l], conv_ln_b[l], conv_width)
        b_out = _hgrn2(proj, hg_lb_logits, hg_norm_g[l], l, 2 * conv_width, hg_width)
        xs = _matmul([a_out, b_out], [w_out_b[:conv_width], w_out_b[conv_width:]], xs, F32,
                     bm=1024, bn=1024)
        h = _rmsnorm(xs, ffn_norm_g[l], BF16)
        u = _swiglu(h, w_gu, d_ff, bm=1024, bf=bf)
        xs = _matmul([u], [w_down_b], xs, F32, bm=1024, bn=512, bk=d_ff // 2)
    out = _rmsnorm(xs, final_norm_g, F32)
    return out.reshape(bsz, seq, d_model)
```

```python
import functools

import numpy as np
import jax
import jax.numpy as jnp
from jax import lax
from jax.experimental import pallas as pl
from jax.experimental.pallas import tpu as pltpu

F32 = jnp.float32
BF16 = jnp.bfloat16
EPS = 1e-6

LANES = 128
SUBLANES = 8
VMEM_BYTES = 64 * 1024 * 1024
VMEM_HEADROOM = 6 * 1024 * 1024

HG_DK = 128
HG_DV = 128
CHUNK = 64
N_LEVELS = 6
CONV_K = 31
CONV_HALO = 32


def _vmem_limit(block_bytes):
    return int(min(VMEM_BYTES - 2 * 1024 * 1024, 2 * block_bytes + VMEM_HEADROOM))


def _sigmoid(x):
    return 1.0 / (1.0 + jnp.exp(-x))


def _rmsnorm_kernel(x_ref, g_ref, o_ref):
    x = x_ref[...]
    ms = jnp.mean(x * x, axis=-1, keepdims=True)
    o_ref[...] = (x * lax.rsqrt(ms + EPS) * g_ref[...]).astype(o_ref.dtype)


def _rmsnorm(x, g, out_dtype, bm=256):
    m, d = x.shape
    return pl.pallas_call(
        _rmsnorm_kernel,
        grid=(m // bm,),
        in_specs=[pl.BlockSpec((bm, d), lambda i: (i, 0)),
                  pl.BlockSpec((1, d), lambda i: (0, 0))],
        out_specs=pl.BlockSpec((bm, d), lambda i: (i, 0)),
        out_shape=jax.ShapeDtypeStruct((m, d), out_dtype),
        compiler_params=pltpu.CompilerParams(
            dimension_semantics=("arbitrary",),
            vmem_limit_bytes=_vmem_limit(bm * d * 8)),
        name="rmsnorm",
    )(x, g.reshape(1, d))


def _matmul_kernel(*refs, n_lhs, has_res, nk):
    lhs = refs[:n_lhs]
    ws = refs[n_lhs:2 * n_lhs]
    res_ref = refs[2 * n_lhs] if has_res else None
    o_ref = refs[-1]
    acc = None
    for a_ref, w_ref in zip(lhs, ws):
        d = jnp.dot(a_ref[...], w_ref[...], preferred_element_type=F32)
        acc = d if acc is None else acc + d
    if nk == 1:
        if has_res:
            acc = acc + res_ref[...]
        o_ref[...] = acc.astype(o_ref.dtype)
    else:
        k = pl.program_id(2)

        @pl.when(k == 0)
        def _():
            o_ref[...] = acc + res_ref[...] if has_res else acc

        @pl.when(k > 0)
        def _():
            o_ref[...] += acc


def _matmul(lhs_list, w, res, out_dtype, bm, bn, bk=None):
    m, kdim = lhs_list[0].shape
    n = w.shape[1]
    bk = kdim if bk is None else bk
    nk = kdim // bk
    n_lhs = len(lhs_list)
    assert m % bm == 0 and n % bn == 0 and kdim % bk == 0 and w.shape[0] == n_lhs * kdim
    assert nk == 1 or out_dtype == F32
    w_spec = lambda p: pl.BlockSpec((bk, bn), lambda i, j, k: (p * nk + k, j))
    in_specs = ([pl.BlockSpec((bm, bk), lambda i, j, k: (i, k))] * n_lhs
                + [w_spec(p) for p in range(n_lhs)])
    args = list(lhs_list) + [w] * n_lhs
    blk = n_lhs * (bm * bk + bk * bn) * 2 + bm * bn * jnp.dtype(out_dtype).itemsize
    if res is not None:
        in_specs.append(pl.BlockSpec((bm, bn), lambda i, j, k: (i, j)))
        args.append(res)
        blk += bm * bn * 4
    return pl.pallas_call(
        functools.partial(_matmul_kernel, n_lhs=n_lhs, has_res=res is not None, nk=nk),
        grid=(m // bm, n // bn, nk),
        in_specs=in_specs,
        out_specs=pl.BlockSpec((bm, bn), lambda i, j, k: (i, j)),
        out_shape=jax.ShapeDtypeStruct((m, n), out_dtype),
        compiler_params=pltpu.CompilerParams(
            dimension_semantics=("arbitrary", "arbitrary", "arbitrary"),
            vmem_limit_bytes=_vmem_limit(blk + bm * bn * 4)),
        name="matmul",
    )(*args)


def _swiglu_kernel(h_ref, wg_ref, wu_ref, u_ref):
    h = h_ref[...]
    g = jnp.dot(h, wg_ref[...], preferred_element_type=F32)
    u = jnp.dot(h, wu_ref[...], preferred_element_type=F32)
    u_ref[...] = (g * _sigmoid(g) * u).astype(u_ref.dtype)


def _swiglu(h, w_gate, w_up, bm, bf):
    m, d = h.shape
    d_ff = w_gate.shape[1]
    assert m % bm == 0 and d_ff % bf == 0
    blk = (bm * d + d * 2 * bf + bm * bf) * 2
    return pl.pallas_call(
        _swiglu_kernel,
        grid=(m // bm, d_ff // bf),
        in_specs=[pl.BlockSpec((bm, d), lambda i, j: (i, 0)),
                  pl.BlockSpec((d, bf), lambda i, j: (0, j)),
                  pl.BlockSpec((d, bf), lambda i, j: (0, j))],
        out_specs=pl.BlockSpec((bm, bf), lambda i, j: (i, j)),
        out_shape=jax.ShapeDtypeStruct((m, d_ff), BF16),
        compiler_params=pltpu.CompilerParams(
            dimension_semantics=("arbitrary", "arbitrary"),
            vmem_limit_bytes=_vmem_limit(blk + bm * 2 * bf * 4)),
        name="swiglu",
    )(h, w_gate, w_up)


CONV_CW = 128
CONV_R = 64


def _conv_kernel(xv_ref, xg_ref, hv_ref, hg_ref, w_ref, b_ref, lg_ref, lb_ref, o_ref,
                 hext_ref, sh_ref, y_ref, *, t_rows, width):
    i = pl.program_id(0)
    hext_ref[CONV_HALO:, :] = xv_ref[...] * _sigmoid(xg_ref[...])
    halo = hv_ref[...] * _sigmoid(hg_ref[...])
    hext_ref[:CONV_HALO, :] = jnp.where(i > 0, halo, 0.0)

    first = CONV_HALO - (CONV_K - 1)
    n_sh = t_rows + CONV_HALO - SUBLANES
    for c in range(width // CONV_CW):
        cs = slice(c * CONV_CW, (c + 1) * CONV_CW)
        for b in range(1, SUBLANES):
            sh_ref[b - 1] = hext_ref[pl.ds(b, n_sh), cs]

        def row_tile(r, carry, cs=cs):
            r0 = pl.multiple_of(r * CONV_R, CONV_R)
            acc = jnp.zeros((CONV_R, CONV_CW), F32)
            for k in range(CONV_K):
                a, b = divmod(first + k, SUBLANES)
                if b == 0:
                    src = hext_ref[pl.ds(r0 + SUBLANES * a, CONV_R), cs]
                else:
                    src = sh_ref[b - 1, pl.ds(r0 + SUBLANES * a, CONV_R), :]
                acc = acc + w_ref[k:k + 1, cs] * src
            y_ref[pl.ds(r0, CONV_R), cs] = acc + b_ref[:, cs]
            return carry

        lax.fori_loop(0, t_rows // CONV_R, row_tile, 0)

    y = y_ref[...]
    mu = jnp.mean(y, axis=-1, keepdims=True)
    yc = y - mu
    var = jnp.mean(yc * yc, axis=-1, keepdims=True)
    z = yc * lax.rsqrt(var + EPS) * lg_ref[...] + lb_ref[...]
    o_ref[...] = (z * _sigmoid(z)).astype(o_ref.dtype)


def _conformer_conv(proj, conv_w, conv_b, ln_g, ln_b, width, t_rows=256):
    s = proj.shape[0]
    hb = t_rows // CONV_HALO
    row = lambda a: a.reshape(1, width)
    halo_map = lambda c: (lambda i: (jnp.maximum(i * hb - 1, 0), c))
    blk = (2 * t_rows + 2 * CONV_HALO + 40) * width * 4 + t_rows * width * 2
    scratch = ((t_rows + CONV_HALO) * width + 7 * (t_rows + CONV_HALO) * CONV_CW + t_rows * width) * 4
    return pl.pallas_call(
        functools.partial(_conv_kernel, t_rows=t_rows, width=width),
        grid=(s // t_rows,),
        in_specs=[pl.BlockSpec((t_rows, width), lambda i: (i, 0)),
                  pl.BlockSpec((t_rows, width), lambda i: (i, 1)),
                  pl.BlockSpec((CONV_HALO, width), halo_map(0)),
                  pl.BlockSpec((CONV_HALO, width), halo_map(1)),
                  pl.BlockSpec((CONV_K, width), lambda i: (0, 0)),
                  pl.BlockSpec((1, width), lambda i: (0, 0)),
                  pl.BlockSpec((1, width), lambda i: (0, 0)),
                  pl.BlockSpec((1, width), lambda i: (0, 0))],
        out_specs=pl.BlockSpec((t_rows, width), lambda i: (i, 0)),
        out_shape=jax.ShapeDtypeStruct((s, width), BF16),
        scratch_shapes=[pltpu.VMEM((t_rows + CONV_HALO, width), F32),
                        pltpu.VMEM((SUBLANES - 1, t_rows + CONV_HALO - SUBLANES, CONV_CW), F32),
                        pltpu.VMEM((t_rows, width), F32)],
        compiler_params=pltpu.CompilerParams(
            dimension_semantics=("arbitrary",),
            vmem_limit_bytes=_vmem_limit(blk + scratch)),
        name="conformer_conv",
    )(proj, proj, proj, proj, conv_w, row(conv_b), row(ln_g), row(ln_b))


def _decay_sum_matrix():
    c = CHUNK
    m = np.zeros((2 + N_LEVELS, c, c), np.float32)
    for t in range(c):
        m[0, t, :t + 1] = 1.0
        m[1, t, t + 1:] = 1.0
        for l in range(N_LEVELS):
            blk = c >> l
            mid = t - t % blk + blk // 2
            if t >= mid:
                m[2 + l, t, mid:t + 1] = 1.0
            else:
                m[2 + l, t, t + 1:mid] = 1.0
    return m.reshape((2 + N_LEVELS) * c, c)


def _hgrn_kernel(q_ref, f_ref, v_ref, g_ref, lbl_ref, ng_ref, m_ref, o_ref, st_ref,
                 *, tb, hb, layer):
    t_blk = pl.program_id(1)

    @pl.when(t_blk == 0)
    def _():
        st_ref[...] = jnp.zeros_like(st_ref)

    c = CHUNK
    lbl = lbl_ref[...]
    e = jnp.exp(lbl - jnp.max(lbl, axis=0, keepdims=True))
    lb_all = jnp.sum(e[:layer + 1], axis=0, keepdims=True) / jnp.sum(e, axis=0, keepdims=True)

    m_all = m_ref[...]
    row = lax.broadcasted_iota(jnp.int32, (c, 1), 0)
    col = lax.broadcasted_iota(jnp.int32, (1, c), 1)
    upper, same_blk = [], []
    for l in range(N_LEVELS):
        blk = c >> l
        upper.append((row % blk) >= (blk // 2))
        same_blk.append((row // blk) == (col // blk))

    nt_dims = (((1,), (1,)), ((), ()))
    tn_dims = (((0,), (0,)), ((), ()))

    for h in range(hb):
        hs = slice(h * HG_DK, (h + 1) * HG_DK)
        lb = lb_all[:, hs]
        ng = ng_ref[:, hs]
        st = st_ref[h]
        for ci in range(tb // c):
            rs = slice(ci * c, (ci + 1) * c)
            q = q_ref[rs, hs]
            v = v_ref[rs, hs]
            g = g_ref[rs, hs]
            qf = q * _sigmoid(q)
            f = lb + (1.0 - lb) * _sigmoid(f_ref[rs, hs])
            kk = 1.0 - f
            log_f = jnp.log(f)
            p0 = log_f.astype(BF16)
            r1 = log_f - p0.astype(F32)
            p1 = r1.astype(BF16)
            p2 = (r1 - p1.astype(F32)).astype(BF16)
            parts = jnp.dot(m_all, jnp.concatenate([p0, p1, p2], axis=1),
                            preferred_element_type=F32)
            ex = parts[:, :HG_DK] + parts[:, HG_DK:2 * HG_DK] + parts[:, 2 * HG_DK:]
            b_cum = ex[:c]
            qe = (qf * jnp.exp(b_cum)).astype(BF16)
            ke = (kk * jnp.exp(ex[c:2 * c])).astype(BF16)
            vb = v.astype(BF16)

            scores = None
            for l in range(N_LEVELS):
                x = jnp.exp(ex[(2 + l) * c:(3 + l) * c])
                qs = jnp.where(upper[l], qf * x, 0.0).astype(BF16)
                ks = jnp.where(upper[l], 0.0, kk * x).astype(BF16)
                s_l = lax.dot_general(qs, ks, nt_dims, preferred_element_type=F32)
                if l > 0:
                    s_l = jnp.where(same_blk[l], s_l, 0.0)
                scores = s_l if scores is None else scores + s_l

            o = jnp.dot(scores.astype(BF16), vb, preferred_element_type=F32)
            o = o + lax.dot_general(qe, st.astype(BF16), nt_dims, preferred_element_type=F32)
            o = o + jnp.sum(qf * kk, axis=-1, keepdims=True) * v

            ut = lax.dot_general(vb, ke, tn_dims, preferred_element_type=F32)
            st = st * jnp.exp(b_cum[c - 1:c, :]) + ut

            ms = jnp.mean(o * o, axis=-1, keepdims=True)
            on = o * lax.rsqrt(ms + EPS) * ng
            o_ref[rs, hs] = (on * (g * _sigmoid(g))).astype(o_ref.dtype)
        st_ref[h] = st


def _hgrn2(proj, lb_logits, norm_g, layer, col0, width, tb=512, hb=2):
    s = proj.shape[0]
    n_heads = width // HG_DK
    bw = hb * HG_DK
    sec = lambda k: (lambda hg, t: (t, (col0 + k * width) // bw + hg))
    depth1 = lb_logits.shape[0]
    m_all = jnp.asarray(_decay_sum_matrix(), BF16)
    blk = 4 * tb * bw * 4 + tb * bw * 2 + m_all.size * 2
    return pl.pallas_call(
        functools.partial(_hgrn_kernel, tb=tb, hb=hb, layer=layer),
        grid=(n_heads // hb, s // tb),
        in_specs=[pl.BlockSpec((tb, bw), sec(0)),
                  pl.BlockSpec((tb, bw), sec(1)),
                  pl.BlockSpec((tb, bw), sec(2)),
                  pl.BlockSpec((tb, bw), sec(3)),
                  pl.BlockSpec((depth1, bw), lambda hg, t: (0, hg)),
                  pl.BlockSpec((1, bw), lambda hg, t: (0, hg)),
                  pl.BlockSpec(m_all.shape, lambda hg, t: (0, 0))],
        out_specs=pl.BlockSpec((tb, bw), lambda hg, t: (t, hg)),
        out_shape=jax.ShapeDtypeStruct((s, width), BF16),
        scratch_shapes=[pltpu.VMEM((hb, HG_DV, HG_DK), F32)],
        compiler_params=pltpu.CompilerParams(
            dimension_semantics=("arbitrary", "arbitrary"),
            vmem_limit_bytes=_vmem_limit(blk + 8 * 1024 * 1024)),
        name="hgrn2",
    )(proj, proj, proj, proj, lb_logits, norm_g.reshape(1, width), m_all)


def kernel(x, attn_norm_g, w_in, conv_w, conv_b, conv_ln_g, conv_ln_b, hg_lb_logits, hg_norm_g,
           w_out, ffn_norm_g, w_gate, w_up, w_down, final_norm_g):
    bsz, seq, d_model = x.shape
    assert bsz == 1
    depth = w_in.shape[0]
    conv_width = conv_w.shape[2]
    hg_width = hg_norm_g.shape[1]
    d_ff = w_gate.shape[2]
    bf = 256
    assert d_ff % bf == 0

    xs = x.reshape(seq, d_model)
    for l in range(depth):
        w_in_b = w_in[l].astype(BF16)
        w_out_b = w_out[l].astype(BF16)
        w_gate_b = w_gate[l].astype(BF16)
        w_up_b = w_up[l].astype(BF16)
        w_down_b = w_down[l].astype(BF16)

        h = _rmsnorm(xs, attn_norm_g[l], BF16)
        proj = _matmul([h], w_in_b, None, F32, bm=1024, bn=1024)
        a_out = _conformer_conv(proj, conv_w[l], conv_b[l], conv_ln_g[l], conv_ln_b[l], conv_width)
        b_out = _hgrn2(proj, hg_lb_logits, hg_norm_g[l], l, 2 * conv_width, hg_width)
        xs = _matmul([a_out, b_out], w_out_b, xs, F32, bm=1024, bn=1024)
        h = _rmsnorm(xs, ffn_norm_g[l], BF16)
        u = _swiglu(h, w_gate_b, w_up_b, bm=1024, bf=bf)
        xs = _matmul([u], w_down_b, xs, F32, bm=1024, bn=512, bk=d_ff // 2)
    out = _rmsnorm(xs, final_norm_g, F32)
    return out.reshape(bsz, seq, d_model)
```

```python
import functools

import numpy as np
import jax
import jax.numpy as jnp
from jax import lax
from jax.experimental import pallas as pl
from jax.experimental.pallas import tpu as pltpu

F32 = jnp.float32
BF16 = jnp.bfloat16
EPS = 1e-6

LANES = 128
SUBLANES = 8
VMEM_BYTES = 64 * 1024 * 1024
VMEM_HEADROOM = 6 * 1024 * 1024

HG_DK = 128
HG_DV = 128
HG_PAIR = 2
CHUNK = 64
N_LEVELS = 6
CONV_K = 31
CONV_HALO = 32


def _vmem_limit(block_bytes):
    return int(min(VMEM_BYTES - 2 * 1024 * 1024, 2 * block_bytes + VMEM_HEADROOM))


def _sigmoid(x):
    return 1.0 / (1.0 + jnp.exp(-x))


def _rmsnorm_kernel(x_ref, g_ref, o_ref):
    x = x_ref[...]
    ms = jnp.mean(x * x, axis=-1, keepdims=True)
    o_ref[...] = (x * lax.rsqrt(ms + EPS) * g_ref[...]).astype(o_ref.dtype)


def _rmsnorm(x, g, out_dtype, bm=256):
    m, d = x.shape
    return pl.pallas_call(
        _rmsnorm_kernel,
        grid=(m // bm,),
        in_specs=[pl.BlockSpec((bm, d), lambda i: (i, 0)),
                  pl.BlockSpec((1, d), lambda i: (0, 0))],
        out_specs=pl.BlockSpec((bm, d), lambda i: (i, 0)),
        out_shape=jax.ShapeDtypeStruct((m, d), out_dtype),
        compiler_params=pltpu.CompilerParams(
            dimension_semantics=("arbitrary",),
            vmem_limit_bytes=_vmem_limit(bm * d * 8)),
        name="rmsnorm",
    )(x, g.reshape(1, d))


def _matmul_kernel(*refs, n_lhs, has_res, nk):
    lhs = refs[:n_lhs]
    ws = refs[n_lhs:2 * n_lhs]
    res_ref = refs[2 * n_lhs] if has_res else None
    o_ref = refs[-1]
    acc = None
    for a_ref, w_ref in zip(lhs, ws):
        d = jnp.dot(a_ref[...], w_ref[...].astype(BF16), preferred_element_type=F32)
        acc = d if acc is None else acc + d
    if nk == 1:
        if has_res:
            acc = acc + res_ref[...]
        o_ref[...] = acc.astype(o_ref.dtype)
    else:
        k = pl.program_id(2)

        @pl.when(k == 0)
        def _():
            o_ref[...] = acc + res_ref[...] if has_res else acc

        @pl.when(k > 0)
        def _():
            o_ref[...] += acc


def _matmul(lhs_list, w, res, out_dtype, bm, bn, bk=None):
    m, kdim = lhs_list[0].shape
    n = w.shape[1]
    bk = kdim if bk is None else bk
    nk = kdim // bk
    n_lhs = len(lhs_list)
    assert m % bm == 0 and n % bn == 0 and kdim % bk == 0 and w.shape[0] == n_lhs * kdim
    assert nk == 1 or out_dtype == F32
    w_bytes = jnp.dtype(w.dtype).itemsize
    w_spec = lambda p: pl.BlockSpec((bk, bn), lambda i, j, k: (p * nk + k, j))
    lhs_mode = pl.Buffered(1) if nk == 1 else None
    in_specs = ([pl.BlockSpec((bm, bk), lambda i, j, k: (i, k), pipeline_mode=lhs_mode)] * n_lhs
                + [w_spec(p) for p in range(n_lhs)])
    args = list(lhs_list) + [w] * n_lhs
    lhs_bufs = 1 if nk == 1 else 2
    blk = (n_lhs * (bm * bk * lhs_bufs + bk * bn * (w_bytes + 1))
           + bm * bn * jnp.dtype(out_dtype).itemsize)
    if res is not None:
        in_specs.append(pl.BlockSpec((bm, bn), lambda i, j, k: (i, j)))
        args.append(res)
        blk += bm * bn * 4
    return pl.pallas_call(
        functools.partial(_matmul_kernel, n_lhs=n_lhs, has_res=res is not None, nk=nk),
        grid=(m // bm, n // bn, nk),
        in_specs=in_specs,
        out_specs=pl.BlockSpec((bm, bn), lambda i, j, k: (i, j)),
        out_shape=jax.ShapeDtypeStruct((m, n), out_dtype),
        compiler_params=pltpu.CompilerParams(
            dimension_semantics=("arbitrary", "arbitrary", "arbitrary"),
            vmem_limit_bytes=_vmem_limit(blk + bm * bn * 4)),
        name="matmul",
    )(*args)


def _swiglu_kernel(h_ref, wg_ref, wu_ref, u_ref):
    h = h_ref[...]
    g = jnp.dot(h, wg_ref[...].astype(BF16), preferred_element_type=F32)
    u = jnp.dot(h, wu_ref[...].astype(BF16), preferred_element_type=F32)
    u_ref[...] = (g * _sigmoid(g) * u).astype(u_ref.dtype)


def _swiglu(h, w_gate, w_up, bm, bf):
    m, d = h.shape
    d_ff = w_gate.shape[1]
    assert m % bm == 0 and d_ff % bf == 0
    w_bytes = jnp.dtype(w_gate.dtype).itemsize
    blk = bm * d + 2 * d * bf * (w_bytes + 1) + bm * bf * 2
    return pl.pallas_call(
        _swiglu_kernel,
        grid=(m // bm, d_ff // bf),
        in_specs=[pl.BlockSpec((bm, d), lambda i, j: (i, 0), pipeline_mode=pl.Buffered(1)),
                  pl.BlockSpec((d, bf), lambda i, j: (0, j)),
                  pl.BlockSpec((d, bf), lambda i, j: (0, j))],
        out_specs=pl.BlockSpec((bm, bf), lambda i, j: (i, j)),
        out_shape=jax.ShapeDtypeStruct((m, d_ff), BF16),
        compiler_params=pltpu.CompilerParams(
            dimension_semantics=("arbitrary", "arbitrary"),
            vmem_limit_bytes=_vmem_limit(blk + bm * 2 * bf * 4)),
        name="swiglu",
    )(h, w_gate, w_up)


CONV_CW = 128
CONV_R = 64


def _conv_kernel(xv_ref, xg_ref, hv_ref, hg_ref, w_ref, b_ref, lg_ref, lb_ref, o_ref,
                 hext_ref, sh_ref, y_ref, *, t_rows, width):
    i = pl.program_id(0)
    hext_ref[CONV_HALO:, :] = xv_ref[...] * _sigmoid(xg_ref[...])
    halo = hv_ref[...] * _sigmoid(hg_ref[...])
    hext_ref[:CONV_HALO, :] = jnp.where(i > 0, halo, 0.0)

    first = CONV_HALO - (CONV_K - 1)
    n_sh = t_rows + CONV_HALO - SUBLANES
    for c in range(width // CONV_CW):
        cs = slice(c * CONV_CW, (c + 1) * CONV_CW)
        for b in range(1, SUBLANES):
            sh_ref[b - 1] = hext_ref[pl.ds(b, n_sh), cs]

        def row_tile(r, carry, cs=cs):
            r0 = pl.multiple_of(r * CONV_R, CONV_R)
            acc = jnp.zeros((CONV_R, CONV_CW), F32)
            for k in range(CONV_K):
                a, b = divmod(first + k, SUBLANES)
                if b == 0:
                    src = hext_ref[pl.ds(r0 + SUBLANES * a, CONV_R), cs]
                else:
                    src = sh_ref[b - 1, pl.ds(r0 + SUBLANES * a, CONV_R), :]
                acc = acc + w_ref[k:k + 1, cs] * src
            y_ref[pl.ds(r0, CONV_R), cs] = acc + b_ref[:, cs]
            return carry

        lax.fori_loop(0, t_rows // CONV_R, row_tile, 0)

    y = y_ref[...]
    mu = jnp.mean(y, axis=-1, keepdims=True)
    yc = y - mu
    var = jnp.mean(yc * yc, axis=-1, keepdims=True)
    z = yc * lax.rsqrt(var + EPS) * lg_ref[...] + lb_ref[...]
    o_ref[...] = (z * _sigmoid(z)).astype(o_ref.dtype)


def _conformer_conv(proj, conv_w, conv_b, ln_g, ln_b, width, t_rows=256):
    s = proj.shape[0]
    hb = t_rows // CONV_HALO
    row = lambda a: a.reshape(1, width)
    halo_map = lambda c: (lambda i: (jnp.maximum(i * hb - 1, 0), c))
    blk = (2 * t_rows + 2 * CONV_HALO + 40) * width * 4 + t_rows * width * 2
    scratch = ((t_rows + CONV_HALO) * width + 7 * (t_rows + CONV_HALO) * CONV_CW + t_rows * width) * 4
    return pl.pallas_call(
        functools.partial(_conv_kernel, t_rows=t_rows, width=width),
        grid=(s // t_rows,),
        in_specs=[pl.BlockSpec((t_rows, width), lambda i: (i, 0)),
                  pl.BlockSpec((t_rows, width), lambda i: (i, 1)),
                  pl.BlockSpec((CONV_HALO, width), halo_map(0)),
                  pl.BlockSpec((CONV_HALO, width), halo_map(1)),
                  pl.BlockSpec((CONV_K, width), lambda i: (0, 0)),
                  pl.BlockSpec((1, width), lambda i: (0, 0)),
                  pl.BlockSpec((1, width), lambda i: (0, 0)),
                  pl.BlockSpec((1, width), lambda i: (0, 0))],
        out_specs=pl.BlockSpec((t_rows, width), lambda i: (i, 0)),
        out_shape=jax.ShapeDtypeStruct((s, width), BF16),
        scratch_shapes=[pltpu.VMEM((t_rows + CONV_HALO, width), F32),
                        pltpu.VMEM((SUBLANES - 1, t_rows + CONV_HALO - SUBLANES, CONV_CW), F32),
                        pltpu.VMEM((t_rows, width), F32)],
        compiler_params=pltpu.CompilerParams(
            dimension_semantics=("arbitrary",),
            vmem_limit_bytes=_vmem_limit(blk + scratch)),
        name="conformer_conv",
    )(proj, proj, proj, proj, conv_w, row(conv_b), row(ln_g), row(ln_b))


HG_MXU_LEVELS = (3, 4)


def _decay_sum_matrix():
    c = CHUNK
    m = np.zeros((1 + len(HG_MXU_LEVELS), c, c), np.float32)
    for t in range(c):
        m[0, t, :t + 1] = 1.0
        for i, l in enumerate(HG_MXU_LEVELS):
            blk = c >> l
            mid = t - t % blk + blk // 2
            if t >= mid:
                m[1 + i, t, mid:t + 1] = 1.0
            else:
                m[1 + i, t, t + 1:mid] = 1.0
    m = m.reshape(-1, c)
    return np.concatenate([m, m], axis=1)


def _hgrn_kernel(q_ref, f_ref, v_ref, g_ref, lbl_ref, ng_ref, m_ref, o_ref, st_ref, *, tb, layer):
    t_blk = pl.program_id(1)

    @pl.when(t_blk == 0)
    def _():
        st_ref[...] = jnp.zeros_like(st_ref)

    c = CHUNK
    w = HG_PAIR * HG_DK
    heads = [slice(h * HG_DK, (h + 1) * HG_DK) for h in range(HG_PAIR)]
    lbl = lbl_ref[...]
    e = jnp.exp(lbl - jnp.max(lbl, axis=0, keepdims=True))
    lb = jnp.sum(e[:layer + 1], axis=0, keepdims=True) / jnp.sum(e, axis=0, keepdims=True)
    ng = ng_ref[...]
    m2 = m_ref[...]

    row = lax.broadcasted_iota(jnp.int32, (c, w), 0)
    upper_row = {l: (row % (c >> l)) >= (c >> (l + 1)) for l in range(3, N_LEVELS)}
    t_i = lax.broadcasted_iota(jnp.int32, (c, HG_PAIR * c), 0)
    s_i = lax.broadcasted_iota(jnp.int32, (c, HG_PAIR * c), 1) % c
    valid = []
    for l in range(N_LEVELS):
        blk = c >> l
        valid.append((t_i // blk == s_i // blk) & (t_i % blk >= blk // 2) & (s_i % blk < blk // 2))

    def blockdiag(a, zero):
        return jnp.concatenate([jnp.concatenate([a[:, heads[0]], zero], axis=1),
                                jnp.concatenate([zero, a[:, heads[1]]], axis=1)], axis=0)

    def per_head(fn):
        return jnp.concatenate([fn(hs) for hs in heads], axis=1)

    zero_c = jnp.zeros((c, HG_DK), BF16)
    zero_s = jnp.zeros((HG_DV, HG_DK), BF16)
    nt_dims = (((1,), (1,)), ((), ()))
    tn_dims = (((0,), (0,)), ((), ()))

    st = [st_ref[h] for h in range(HG_PAIR)]
    for ci in range(tb // c):
        rs = slice(ci * c, (ci + 1) * c)
        q = q_ref[rs, :]
        v = v_ref[rs, :]
        g = g_ref[rs, :]
        qf = q * _sigmoid(q)
        f = lb + (1.0 - lb) * _sigmoid(f_ref[rs, :])
        kk = 1.0 - f
        log_f = jnp.log(f)
        p0 = log_f.astype(BF16)
        p1 = (log_f - p0.astype(F32)).astype(BF16)
        ex = jnp.dot(m2, jnp.concatenate([p0, p1], axis=0), preferred_element_type=F32)
        b_cum = ex[:c]
        b_last = b_cum[c - 1:c, :]

        scores = None
        for l in range(N_LEVELS):
            blk = c >> l
            half = blk // 2
            if blk >= 2 * SUBLANES:
                ref = jnp.concatenate([jnp.broadcast_to(b_cum[s0 + half - 1:s0 + half, :], (blk, w))
                                       for s0 in range(0, c, blk)], axis=0)
                sel = jnp.concatenate([p[s0 + o:s0 + o + half, :] for s0 in range(0, c, blk)
                                       for p, o in ((kk, 0), (qf, half))], axis=0)
                y = sel * jnp.exp(-jnp.abs(b_cum - ref))
            elif l in HG_MXU_LEVELS:
                i = HG_MXU_LEVELS.index(l)
                y = jnp.where(upper_row[l], qf, kk) * jnp.exp(ex[(1 + i) * c:(2 + i) * c])
            else:
                y = jnp.where(upper_row[l], qf * f, kk)
            yb = y.astype(BF16)
            s_l = lax.dot_general(yb, blockdiag(yb, zero_c), nt_dims, preferred_element_type=F32)
            s_l = jnp.where(valid[l], s_l, 0.0)
            scores = s_l if scores is None else scores + s_l

        vb = v.astype(BF16)
        o = jnp.dot(scores.astype(BF16), blockdiag(vb, zero_c), preferred_element_type=F32)
        qe = (qf * jnp.exp(b_cum)).astype(BF16)
        st_bd = jnp.concatenate(
            [jnp.concatenate([st[0].astype(BF16), zero_s], axis=1),
             jnp.concatenate([zero_s, st[1].astype(BF16)], axis=1)], axis=0)
        o = o + lax.dot_general(qe, st_bd, nt_dims, preferred_element_type=F32)
        qk = qf * kk
        o = o + per_head(lambda hs: jnp.broadcast_to(
            jnp.sum(qk[:, hs], axis=-1, keepdims=True), (c, HG_DK))) * v

        ke = (kk * jnp.exp(b_last - b_cum)).astype(BF16)
        dec = jnp.exp(b_last)
        for h, hs in enumerate(heads):
            ut = lax.dot_general(vb[:, hs], ke[:, hs], tn_dims, preferred_element_type=F32)
            st[h] = st[h] * dec[:, hs] + ut

        o2 = o * o
        inv = per_head(lambda hs: jnp.broadcast_to(
            lax.rsqrt(jnp.mean(o2[:, hs], axis=-1, keepdims=True) + EPS), (c, HG_DV)))
        o_ref[rs, :] = (o * inv * ng * (g * _sigmoid(g))).astype(o_ref.dtype)

    for h in range(HG_PAIR):
        st_ref[h] = st[h]


def _hgrn2(proj, lb_logits, norm_g, layer, col0, width, tb=512):
    s = proj.shape[0]
    n_heads = width // HG_DK
    bw = HG_PAIR * HG_DK
    assert n_heads % HG_PAIR == 0 and s % tb == 0 and tb % CHUNK == 0 and col0 % bw == 0
    sec = lambda k: (lambda hg, t: (t, (col0 + k * width) // bw + hg))
    depth1 = lb_logits.shape[0]
    m2 = jnp.asarray(_decay_sum_matrix(), BF16)
    blk = 4 * tb * bw * 4 + tb * bw * 2 + m2.size * 2
    return pl.pallas_call(
        functools.partial(_hgrn_kernel, tb=tb, layer=layer),
        grid=(n_heads // HG_PAIR, s // tb),
        in_specs=[pl.BlockSpec((tb, bw), sec(0)),
                  pl.BlockSpec((tb, bw), sec(1)),
                  pl.BlockSpec((tb, bw), sec(2)),
                  pl.BlockSpec((tb, bw), sec(3)),
                  pl.BlockSpec((depth1, bw), lambda hg, t: (0, hg)),
                  pl.BlockSpec((1, bw), lambda hg, t: (0, hg)),
                  pl.BlockSpec(m2.shape, lambda hg, t: (0, 0))],
        out_specs=pl.BlockSpec((tb, bw), lambda hg, t: (t, hg)),
        out_shape=jax.ShapeDtypeStruct((s, width), BF16),
        scratch_shapes=[pltpu.VMEM((HG_PAIR, HG_DV, HG_DK), F32)],
        compiler_params=pltpu.CompilerParams(
            dimension_semantics=("arbitrary", "arbitrary"),
            vmem_limit_bytes=_vmem_limit(blk + 8 * 1024 * 1024)),
        name="hgrn2",
    )(proj, proj, proj, proj, lb_logits, norm_g.reshape(1, width), m2)


def kernel(x, attn_norm_g, w_in, conv_w, conv_b, conv_ln_g, conv_ln_b, hg_lb_logits, hg_norm_g,
           w_out, ffn_norm_g, w_gate, w_up, w_down, final_norm_g):
    bsz, seq, d_model = x.shape
    assert bsz == 1
    depth = w_in.shape[0]
    conv_width = conv_w.shape[2]
    hg_width = hg_norm_g.shape[1]
    d_ff = w_gate.shape[2]
    bf = 256
    assert d_ff % bf == 0

    xs = x.reshape(seq, d_model)
    for l in range(depth):
        w_out_b = w_out[l].astype(BF16)
        w_down_b = w_down[l].astype(BF16)

        h = _rmsnorm(xs, attn_norm_g[l], BF16)
        proj = _matmul([h], w_in[l], None, F32, bm=2048, bn=512)
        a_out = _conformer_conv(proj, conv_w[l], conv_b[l], conv_ln_g[l], conv_ln_b[l], conv_width)
        b_out = _hgrn2(proj, hg_lb_logits, hg_norm_g[l], l, 2 * conv_width, hg_width)
        xs = _matmul([a_out, b_out], w_out_b, xs, F32, bm=1024, bn=1024)
        h = _rmsnorm(xs, ffn_norm_g[l], BF16)
        u = _swiglu(h, w_gate[l], w_up[l], bm=2048, bf=bf)
        xs = _matmul([u], w_down_b, xs, F32, bm=1024, bn=512, bk=d_ff // 2)
    out = _rmsnorm(xs, final_norm_g, F32)
    return out.reshape(bsz, seq, d_model)
```

```python
import functools

import numpy as np
import jax
import jax.numpy as jnp
from jax import lax
from jax.experimental import pallas as pl
from jax.experimental.pallas import tpu as pltpu

F32 = jnp.float32
BF16 = jnp.bfloat16
EPS = 1e-6

LANES = 128
SUBLANES = 8
VMEM_BYTES = 64 * 1024 * 1024
VMEM_HEADROOM = 6 * 1024 * 1024

HG_DK = 128
HG_DV = 128
HG_PAIR = 2
CHUNK = 64
N_LEVELS = 6
CONV_K = 31
CONV_HALO = 32


def _vmem_limit(block_bytes):
    return int(min(VMEM_BYTES - 2 * 1024 * 1024, 2 * block_bytes + VMEM_HEADROOM))


def _sigmoid(x):
    return 1.0 / (1.0 + jnp.exp(-x))


def _rmsnorm_kernel(x_ref, g_ref, o_ref):
    x = x_ref[...]
    ms = jnp.mean(x * x, axis=-1, keepdims=True)
    o_ref[...] = (x * lax.rsqrt(ms + EPS) * g_ref[...]).astype(o_ref.dtype)


def _rmsnorm(x, g, out_dtype, bm=256):
    m, d = x.shape
    return pl.pallas_call(
        _rmsnorm_kernel,
        grid=(m // bm,),
        in_specs=[pl.BlockSpec((bm, d), lambda i: (i, 0)),
                  pl.BlockSpec((1, d), lambda i: (0, 0))],
        out_specs=pl.BlockSpec((bm, d), lambda i: (i, 0)),
        out_shape=jax.ShapeDtypeStruct((m, d), out_dtype),
        compiler_params=pltpu.CompilerParams(
            dimension_semantics=("arbitrary",),
            vmem_limit_bytes=_vmem_limit(bm * d * 8)),
        name="rmsnorm",
    )(x, g.reshape(1, d))


def _matmul_kernel(*refs, n_lhs, has_res, nk):
    lhs = refs[:n_lhs]
    ws = refs[n_lhs:2 * n_lhs]
    res_ref = refs[2 * n_lhs] if has_res else None
    o_ref = refs[-1]
    acc = None
    for a_ref, w_ref in zip(lhs, ws):
        d = jnp.dot(a_ref[...], w_ref[...].astype(BF16), preferred_element_type=F32)
        acc = d if acc is None else acc + d
    if nk == 1:
        if has_res:
            acc = acc + res_ref[...]
        o_ref[...] = acc.astype(o_ref.dtype)
    else:
        k = pl.program_id(2)

        @pl.when(k == 0)
        def _():
            o_ref[...] = acc + res_ref[...] if has_res else acc

        @pl.when(k > 0)
        def _():
            o_ref[...] += acc


def _matmul(lhs_list, w, res, out_dtype, bm, bn, bk=None, lhs_bufs=2):
    m, kdim = lhs_list[0].shape
    n = w.shape[1]
    bk = kdim if bk is None else bk
    nk = kdim // bk
    n_lhs = len(lhs_list)
    assert m % bm == 0 and n % bn == 0 and kdim % bk == 0 and w.shape[0] == n_lhs * kdim
    assert nk == 1 or out_dtype == F32
    w_bytes = jnp.dtype(w.dtype).itemsize
    w_spec = lambda p: pl.BlockSpec((bk, bn), lambda i, j, k: (p * nk + k, j))
    assert lhs_bufs == 2 or nk == 1
    in_specs = ([pl.BlockSpec((bm, bk), lambda i, j, k: (i, k), pipeline_mode=pl.Buffered(lhs_bufs))] * n_lhs
                + [w_spec(p) for p in range(n_lhs)])
    args = list(lhs_list) + [w] * n_lhs
    w_tmp = 1 if w.dtype != BF16 else 0
    blk = (n_lhs * (bm * bk * lhs_bufs + bk * bn * (w_bytes + w_tmp))
           + bm * bn * jnp.dtype(out_dtype).itemsize)
    if res is not None:
        in_specs.append(pl.BlockSpec((bm, bn), lambda i, j, k: (i, j)))
        args.append(res)
        blk += bm * bn * 4
    return pl.pallas_call(
        functools.partial(_matmul_kernel, n_lhs=n_lhs, has_res=res is not None, nk=nk),
        grid=(m // bm, n // bn, nk),
        in_specs=in_specs,
        out_specs=pl.BlockSpec((bm, bn), lambda i, j, k: (i, j)),
        out_shape=jax.ShapeDtypeStruct((m, n), out_dtype),
        compiler_params=pltpu.CompilerParams(
            dimension_semantics=("arbitrary", "arbitrary", "arbitrary"),
            vmem_limit_bytes=_vmem_limit(blk + bm * bn * 4)),
        name="matmul",
    )(*args)


def _swiglu_kernel(h_ref, wg_ref, wu_ref, u_ref):
    h = h_ref[...]
    g = jnp.dot(h, wg_ref[...].astype(BF16), preferred_element_type=F32)
    u = jnp.dot(h, wu_ref[...].astype(BF16), preferred_element_type=F32)
    u_ref[...] = (g * _sigmoid(g) * u).astype(u_ref.dtype)


def _swiglu(h, w_gate, w_up, bm, bf):
    m, d = h.shape
    d_ff = w_gate.shape[1]
    assert m % bm == 0 and d_ff % bf == 0
    w_bytes = jnp.dtype(w_gate.dtype).itemsize
    blk = bm * d + 2 * d * bf * (w_bytes + 1) + bm * bf * 2
    return pl.pallas_call(
        _swiglu_kernel,
        grid=(m // bm, d_ff // bf),
        in_specs=[pl.BlockSpec((bm, d), lambda i, j: (i, 0), pipeline_mode=pl.Buffered(1)),
                  pl.BlockSpec((d, bf), lambda i, j: (0, j)),
                  pl.BlockSpec((d, bf), lambda i, j: (0, j))],
        out_specs=pl.BlockSpec((bm, bf), lambda i, j: (i, j)),
        out_shape=jax.ShapeDtypeStruct((m, d_ff), BF16),
        compiler_params=pltpu.CompilerParams(
            dimension_semantics=("arbitrary", "arbitrary"),
            vmem_limit_bytes=_vmem_limit(blk + bm * 2 * bf * 4)),
        name="swiglu",
    )(h, w_gate, w_up)


CONV_CW = 128
CONV_R = 64


def _conv_kernel(xv_ref, xg_ref, hv_ref, hg_ref, w_ref, b_ref, lg_ref, lb_ref, o_ref,
                 hext_ref, sh_ref, y_ref, *, t_rows, width):
    i = pl.program_id(0)
    hext_ref[CONV_HALO:, :] = xv_ref[...] * _sigmoid(xg_ref[...])
    halo = hv_ref[...] * _sigmoid(hg_ref[...])
    hext_ref[:CONV_HALO, :] = jnp.where(i > 0, halo, 0.0)

    first = CONV_HALO - (CONV_K - 1)
    n_sh = t_rows + CONV_HALO - SUBLANES
    for c in range(width // CONV_CW):
        cs = slice(c * CONV_CW, (c + 1) * CONV_CW)
        for b in range(1, SUBLANES):
            sh_ref[b - 1] = hext_ref[pl.ds(b, n_sh), cs]

        def row_tile(r, carry, cs=cs):
            r0 = pl.multiple_of(r * CONV_R, CONV_R)
            acc = jnp.zeros((CONV_R, CONV_CW), F32)
            for k in range(CONV_K):
                a, b = divmod(first + k, SUBLANES)
                if b == 0:
                    src = hext_ref[pl.ds(r0 + SUBLANES * a, CONV_R), cs]
                else:
                    src = sh_ref[b - 1, pl.ds(r0 + SUBLANES * a, CONV_R), :]
                acc = acc + w_ref[k:k + 1, cs] * src
            y_ref[pl.ds(r0, CONV_R), cs] = acc + b_ref[:, cs]
            return carry

        lax.fori_loop(0, t_rows // CONV_R, row_tile, 0)

    y = y_ref[...]
    mu = jnp.mean(y, axis=-1, keepdims=True)
    yc = y - mu
    var = jnp.mean(yc * yc, axis=-1, keepdims=True)
    z = yc * lax.rsqrt(var + EPS) * lg_ref[...] + lb_ref[...]
    o_ref[...] = (z * _sigmoid(z)).astype(o_ref.dtype)


def _conformer_conv(proj, conv_w, conv_b, ln_g, ln_b, width, t_rows=256):
    s = proj.shape[0]
    hb = t_rows // CONV_HALO
    row = lambda a: a.reshape(1, width)
    halo_map = lambda c: (lambda i: (jnp.maximum(i * hb - 1, 0), c))
    blk = (2 * t_rows + 2 * CONV_HALO + 40) * width * 4 + t_rows * width * 2
    scratch = ((t_rows + CONV_HALO) * width + 7 * (t_rows + CONV_HALO) * CONV_CW + t_rows * width) * 4
    return pl.pallas_call(
        functools.partial(_conv_kernel, t_rows=t_rows, width=width),
        grid=(s // t_rows,),
        in_specs=[pl.BlockSpec((t_rows, width), lambda i: (i, 0)),
                  pl.BlockSpec((t_rows, width), lambda i: (i, 1)),
                  pl.BlockSpec((CONV_HALO, width), halo_map(0)),
                  pl.BlockSpec((CONV_HALO, width), halo_map(1)),
                  pl.BlockSpec((CONV_K, width), lambda i: (0, 0)),
                  pl.BlockSpec((1, width), lambda i: (0, 0)),
                  pl.BlockSpec((1, width), lambda i: (0, 0)),
                  pl.BlockSpec((1, width), lambda i: (0, 0))],
        out_specs=pl.BlockSpec((t_rows, width), lambda i: (i, 0)),
        out_shape=jax.ShapeDtypeStruct((s, width), BF16),
        scratch_shapes=[pltpu.VMEM((t_rows + CONV_HALO, width), F32),
                        pltpu.VMEM((SUBLANES - 1, t_rows + CONV_HALO - SUBLANES, CONV_CW), F32),
                        pltpu.VMEM((t_rows, width), F32)],
        compiler_params=pltpu.CompilerParams(
            dimension_semantics=("arbitrary",),
            vmem_limit_bytes=_vmem_limit(blk + scratch)),
        name="conformer_conv",
    )(proj, proj, proj, proj, conv_w, row(conv_b), row(ln_g), row(ln_b))


HG_MXU_LEVELS = (3, 4)


def _decay_sum_matrix():
    c = CHUNK
    m = np.zeros((1 + len(HG_MXU_LEVELS), c, c), np.float32)
    for t in range(c):
        m[0, t, :t + 1] = 1.0
        for i, l in enumerate(HG_MXU_LEVELS):
            blk = c >> l
            mid = t - t % blk + blk // 2
            if t >= mid:
                m[1 + i, t, mid:t + 1] = 1.0
            else:
                m[1 + i, t, t + 1:mid] = 1.0
    m = m.reshape(-1, c)
    return np.concatenate([m, m], axis=1)


def _hgrn_kernel(q_ref, f_ref, v_ref, g_ref, lbl_ref, ng_ref, m_ref, a_ref, wa_ref, wb_ref, wl_ref,
                 x_ref, fg_ref, x1_ref, h2_ref, st_ref, bprev_ref, bnew_ref, *, tb, layer, n_pairs):
    t_blk = pl.program_id(0)
    hp = pl.program_id(1)

    @pl.when(t_blk == 0)
    def _():
        st_ref[pl.ds(HG_PAIR * hp, HG_PAIR)] = jnp.zeros((HG_PAIR, HG_DV, HG_DK), F32)

    @pl.when(hp == 0)
    def _():
        x1_ref[...] = x_ref[...]
        bprev_ref[...] = jnp.zeros_like(bprev_ref)

    x1_ref[...] += (jnp.dot(a_ref[...], wa_ref[...], preferred_element_type=F32)
                    + jnp.dot(bprev_ref[...], wb_ref[...], preferred_element_type=F32))

    c = CHUNK
    n_chunks = tb // c
    w = HG_PAIR * HG_DK
    heads = [slice(h * HG_DK, (h + 1) * HG_DK) for h in range(HG_PAIR)]
    lbl = lbl_ref[...]
    e = jnp.exp(lbl - jnp.max(lbl, axis=0, keepdims=True))
    lb = jnp.sum(e[:layer + 1], axis=0, keepdims=True) / jnp.sum(e, axis=0, keepdims=True)
    ng = ng_ref[...]
    m2 = m_ref[...]

    row = lax.broadcasted_iota(jnp.int32, (c, w), 0)
    upper_row = {l: (row % (c >> l)) >= (c >> (l + 1)) for l in range(3, N_LEVELS)}
    t_i = lax.broadcasted_iota(jnp.int32, (c, HG_PAIR * c), 0)
    s_i = lax.broadcasted_iota(jnp.int32, (c, HG_PAIR * c), 1) % c
    valid = []
    for l in range(N_LEVELS):
        blk = c >> l
        valid.append((t_i // blk == s_i // blk) & (t_i % blk >= blk // 2) & (s_i % blk < blk // 2))

    def blockdiag(a, zero):
        return jnp.concatenate([jnp.concatenate([a[:, heads[0]], zero], axis=1),
                                jnp.concatenate([zero, a[:, heads[1]]], axis=1)], axis=0)

    def per_head(fn):
        return jnp.concatenate([fn(hs) for hs in heads], axis=1)

    zero_c = jnp.zeros((c, HG_DK), BF16)
    zero_s = jnp.zeros((HG_DV, HG_DK), BF16)
    nt_dims = (((1,), (1,)), ((), ()))
    tn_dims = (((0,), (0,)), ((), ()))

    st = [st_ref[HG_PAIR * hp + h] for h in range(HG_PAIR)]
    for ci in range(n_chunks):
        rs = slice(ci * c, (ci + 1) * c)
        q = q_ref[rs, :]
        v = v_ref[rs, :]
        g = g_ref[rs, :]
        qf = q * _sigmoid(q)
        f = lb + (1.0 - lb) * _sigmoid(f_ref[rs, :])
        kk = 1.0 - f
        log_f = jnp.log(f)
        p0 = log_f.astype(BF16)
        p1 = (log_f - p0.astype(F32)).astype(BF16)
        ex = jnp.dot(m2, jnp.concatenate([p0, p1], axis=0), preferred_element_type=F32)
        b_cum = ex[:c]
        b_last = b_cum[c - 1:c, :]

        scores = None
        for l in range(N_LEVELS):
            blk = c >> l
            half = blk // 2
            if blk >= 2 * SUBLANES:
                ref = jnp.concatenate([jnp.broadcast_to(b_cum[s0 + half - 1:s0 + half, :], (blk, w))
                                       for s0 in range(0, c, blk)], axis=0)
                sel = jnp.concatenate([p[s0 + o:s0 + o + half, :] for s0 in range(0, c, blk)
                                       for p, o in ((kk, 0), (qf, half))], axis=0)
                y = sel * jnp.exp(-jnp.abs(b_cum - ref))
            elif l in HG_MXU_LEVELS:
                i = HG_MXU_LEVELS.index(l)
                y = jnp.where(upper_row[l], qf, kk) * jnp.exp(ex[(1 + i) * c:(2 + i) * c])
            else:
                y = jnp.where(upper_row[l], qf * f, kk)
            yb = y.astype(BF16)
            s_l = lax.dot_general(yb, blockdiag(yb, zero_c), nt_dims, preferred_element_type=F32)
            s_l = jnp.where(valid[l], s_l, 0.0)
            scores = s_l if scores is None else scores + s_l

        vb = v.astype(BF16)
        o = jnp.dot(scores.astype(BF16), blockdiag(vb, zero_c), preferred_element_type=F32)
        qe = (qf * jnp.exp(b_cum)).astype(BF16)
        st_bd = jnp.concatenate(
            [jnp.concatenate([st[0].astype(BF16), zero_s], axis=1),
             jnp.concatenate([zero_s, st[1].astype(BF16)], axis=1)], axis=0)
        o = o + lax.dot_general(qe, st_bd, nt_dims, preferred_element_type=F32)
        qk = qf * kk
        o = o + per_head(lambda hs: jnp.broadcast_to(
            jnp.sum(qk[:, hs], axis=-1, keepdims=True), (c, HG_DK))) * v

        ke = (kk * jnp.exp(b_last - b_cum)).astype(BF16)
        dec = jnp.exp(b_last)
        for h, hs in enumerate(heads):
            ut = lax.dot_general(vb[:, hs], ke[:, hs], tn_dims, preferred_element_type=F32)
            st[h] = st[h] * dec[:, hs] + ut

        o2 = o * o
        inv = per_head(lambda hs: jnp.broadcast_to(
            lax.rsqrt(jnp.mean(o2[:, hs], axis=-1, keepdims=True) + EPS), (c, HG_DV)))
        bnew_ref[rs, :] = (o * inv * ng * (g * _sigmoid(g))).astype(bnew_ref.dtype)

    for h in range(HG_PAIR):
        st_ref[HG_PAIR * hp + h] = st[h]
    bprev_ref[...] = bnew_ref[...]

    @pl.when(hp == n_pairs - 1)
    def _():
        x1 = x1_ref[...] + jnp.dot(bnew_ref[...], wl_ref[...], preferred_element_type=F32)
        x1_ref[...] = x1
        ms = jnp.mean(x1 * x1, axis=-1, keepdims=True)
        h2_ref[...] = (x1 * lax.rsqrt(ms + EPS) * fg_ref[...]).astype(h2_ref.dtype)


def _hgrn2_outproj(proj, lb_logits, norm_g, layer, col0, width, a_out, w_out, x_res, ffn_g, tb=512):
    s, d_model = x_res.shape
    conv_width = a_out.shape[1]
    n_heads = width // HG_DK
    bw = HG_PAIR * HG_DK
    n_pairs = n_heads // HG_PAIR
    assert n_heads % HG_PAIR == 0 and s % tb == 0 and tb % CHUNK == 0 and col0 % bw == 0
    assert conv_width == n_pairs * bw and w_out.shape == (conv_width + width, d_model)
    sec = lambda k: (lambda t, hp: (t, (col0 + k * width) // bw + hp))
    depth1 = lb_logits.shape[0]
    m2 = jnp.asarray(_decay_sum_matrix(), BF16)
    blk = (4 * tb * bw * 4 + tb * bw * 2 + m2.size * 2 + 2 * bw * d_model * 2
           + (bw * d_model * 2 + tb * d_model * 4) // 2 + tb * d_model * (4 + 2)
           + (n_heads * HG_DV * HG_DK * 4 + tb * bw * 2) // 2
           + tb * d_model * 4 // 4)
    return pl.pallas_call(
        functools.partial(_hgrn_kernel, tb=tb, layer=layer, n_pairs=n_pairs),
        grid=(s // tb, n_pairs),
        in_specs=[pl.BlockSpec((tb, bw), sec(0)),
                  pl.BlockSpec((tb, bw), sec(1)),
                  pl.BlockSpec((tb, bw), sec(2)),
                  pl.BlockSpec((tb, bw), sec(3)),
                  pl.BlockSpec((depth1, bw), lambda t, hp: (0, hp)),
                  pl.BlockSpec((1, bw), lambda t, hp: (0, hp)),
                  pl.BlockSpec(m2.shape, lambda t, hp: (0, 0)),
                  pl.BlockSpec((tb, bw), lambda t, hp: (t, hp)),
                  pl.BlockSpec((bw, d_model), lambda t, hp: (hp, 0)),
                  pl.BlockSpec((bw, d_model), lambda t, hp: (n_pairs + jnp.maximum(hp - 1, 0), 0)),
                  pl.BlockSpec((bw, d_model), lambda t, hp: (2 * n_pairs - 1, 0),
                               pipeline_mode=pl.Buffered(1)),
                  pl.BlockSpec((tb, d_model), lambda t, hp: (t, 0), pipeline_mode=pl.Buffered(1)),
                  pl.BlockSpec((1, d_model), lambda t, hp: (0, 0))],
        out_specs=[pl.BlockSpec((tb, d_model), lambda t, hp: (t, 0)),
                   pl.BlockSpec((tb, d_model), lambda t, hp: (t, 0))],
        out_shape=[jax.ShapeDtypeStruct((s, d_model), F32),
                   jax.ShapeDtypeStruct((s, d_model), BF16)],
        scratch_shapes=[pltpu.VMEM((n_heads, HG_DV, HG_DK), F32),
                        pltpu.VMEM((tb, bw), BF16),
                        pltpu.VMEM((tb, bw), BF16)],
        compiler_params=pltpu.CompilerParams(
            dimension_semantics=("arbitrary", "arbitrary"),
            vmem_limit_bytes=_vmem_limit(blk)),
        name="hgrn2_outproj",
    )(proj, proj, proj, proj, lb_logits, norm_g.reshape(1, width), m2,
      a_out, w_out, w_out, w_out, x_res, ffn_g.reshape(1, d_model))


def kernel(x, attn_norm_g, w_in, conv_w, conv_b, conv_ln_g, conv_ln_b, hg_lb_logits, hg_norm_g,
           w_out, ffn_norm_g, w_gate, w_up, w_down, final_norm_g):
    bsz, seq, d_model = x.shape
    assert bsz == 1
    depth = w_in.shape[0]
    conv_width = conv_w.shape[2]
    hg_width = hg_norm_g.shape[1]
    d_ff = w_gate.shape[2]
    bf = 256
    assert d_ff % bf == 0

    xs = x.reshape(seq, d_model)
    for l in range(depth):
        w_out_b = w_out[l].astype(BF16)
        w_down_b = w_down[l].astype(BF16)

        h = _rmsnorm(xs, attn_norm_g[l], BF16)
        proj = _matmul([h], w_in[l], None, F32, bm=2048, bn=512, lhs_bufs=1)
        a_out = _conformer_conv(proj, conv_w[l], conv_b[l], conv_ln_g[l], conv_ln_b[l], conv_width)
        xs, h = _hgrn2_outproj(proj, hg_lb_logits, hg_norm_g[l], l, 2 * conv_width, hg_width,
                               a_out, w_out_b, xs, ffn_norm_g[l])
        u = _swiglu(h, w_gate[l], w_up[l], bm=2048, bf=bf)
        xs = _matmul([u], w_down_b, xs, F32, bm=512, bn=512)
    out = _rmsnorm(xs, final_norm_g, F32)
    return out.reshape(bsz, seq, d_model)
```

```python
import functools

import numpy as np
import jax
import jax.numpy as jnp
from jax import lax
from jax.experimental import pallas as pl
from jax.experimental.pallas import tpu as pltpu

F32 = jnp.float32
BF16 = jnp.bfloat16
EPS = 1e-6

LANES = 128
SUBLANES = 8
VMEM_BYTES = 64 * 1024 * 1024
VMEM_HEADROOM = 6 * 1024 * 1024

HG_DK = 128
HG_DV = 128
HG_PAIR = 2
CHUNK = 64
N_LEVELS = 6
CONV_K = 31
CONV_HALO = 32


def _vmem_limit(block_bytes):
    return int(min(VMEM_BYTES - 2 * 1024 * 1024, 2 * block_bytes + VMEM_HEADROOM))


def _sigmoid(x):
    return 1.0 / (1.0 + jnp.exp(-x))


def _rmsnorm_kernel(x_ref, g_ref, o_ref):
    x = x_ref[...]
    ms = jnp.mean(x * x, axis=-1, keepdims=True)
    o_ref[...] = (x * lax.rsqrt(ms + EPS) * g_ref[...]).astype(o_ref.dtype)


def _rmsnorm(x, g, out_dtype, bm=256):
    m, d = x.shape
    return pl.pallas_call(
        _rmsnorm_kernel,
        grid=(m // bm,),
        in_specs=[pl.BlockSpec((bm, d), lambda i: (i, 0)),
                  pl.BlockSpec((1, d), lambda i: (0, 0))],
        out_specs=pl.BlockSpec((bm, d), lambda i: (i, 0)),
        out_shape=jax.ShapeDtypeStruct((m, d), out_dtype),
        compiler_params=pltpu.CompilerParams(
            dimension_semantics=("arbitrary",),
            vmem_limit_bytes=_vmem_limit(bm * d * 8)),
        name="rmsnorm",
    )(x, g.reshape(1, d))


def _matmul_kernel(*refs, n_lhs, has_res, nk):
    lhs = refs[:n_lhs]
    ws = refs[n_lhs:2 * n_lhs]
    res_ref = refs[2 * n_lhs] if has_res else None
    o_ref = refs[-1]
    acc = None
    for a_ref, w_ref in zip(lhs, ws):
        d = jnp.dot(a_ref[...], w_ref[...].astype(BF16), preferred_element_type=F32)
        acc = d if acc is None else acc + d
    if nk == 1:
        if has_res:
            acc = acc + res_ref[...]
        o_ref[...] = acc.astype(o_ref.dtype)
    else:
        k = pl.program_id(2)

        @pl.when(k == 0)
        def _():
            o_ref[...] = acc + res_ref[...] if has_res else acc

        @pl.when(k > 0)
        def _():
            o_ref[...] += acc


def _matmul(lhs_list, w, res, out_dtype, bm, bn, bk=None, lhs_bufs=2):
    m, kdim = lhs_list[0].shape
    n = w.shape[1]
    bk = kdim if bk is None else bk
    nk = kdim // bk
    n_lhs = len(lhs_list)
    assert m % bm == 0 and n % bn == 0 and kdim % bk == 0 and w.shape[0] == n_lhs * kdim
    assert nk == 1 or out_dtype == F32
    w_bytes = jnp.dtype(w.dtype).itemsize
    w_spec = lambda p: pl.BlockSpec((bk, bn), lambda i, j, k: (p * nk + k, j))
    assert lhs_bufs == 2 or nk == 1
    in_specs = ([pl.BlockSpec((bm, bk), lambda i, j, k: (i, k), pipeline_mode=pl.Buffered(lhs_bufs))] * n_lhs
                + [w_spec(p) for p in range(n_lhs)])
    args = list(lhs_list) + [w] * n_lhs
    w_tmp = 1 if w.dtype != BF16 else 0
    blk = (n_lhs * (bm * bk * lhs_bufs + bk * bn * (w_bytes + w_tmp))
           + bm * bn * jnp.dtype(out_dtype).itemsize)
    if res is not None:
        in_specs.append(pl.BlockSpec((bm, bn), lambda i, j, k: (i, j)))
        args.append(res)
        blk += bm * bn * 4
    return pl.pallas_call(
        functools.partial(_matmul_kernel, n_lhs=n_lhs, has_res=res is not None, nk=nk),
        grid=(m // bm, n // bn, nk),
        in_specs=in_specs,
        out_specs=pl.BlockSpec((bm, bn), lambda i, j, k: (i, j)),
        out_shape=jax.ShapeDtypeStruct((m, n), out_dtype),
        compiler_params=pltpu.CompilerParams(
            dimension_semantics=("arbitrary", "arbitrary", "arbitrary"),
            vmem_limit_bytes=_vmem_limit(blk + bm * bn * 4)),
        name="matmul",
    )(*args)


def _swiglu_kernel(h_ref, wg_ref, wu_ref, u_ref):
    h = h_ref[...]
    g = jnp.dot(h, wg_ref[...].astype(BF16), preferred_element_type=F32)
    u = jnp.dot(h, wu_ref[...].astype(BF16), preferred_element_type=F32)
    u_ref[...] = (g * _sigmoid(g) * u).astype(u_ref.dtype)


def _swiglu(h, w_gate, w_up, bm, bf):
    m, d = h.shape
    d_ff = w_gate.shape[1]
    assert m % bm == 0 and d_ff % bf == 0
    w_bytes = jnp.dtype(w_gate.dtype).itemsize
    blk = bm * d + 2 * d * bf * (w_bytes + 1) + bm * bf * 2
    return pl.pallas_call(
        _swiglu_kernel,
        grid=(m // bm, d_ff // bf),
        in_specs=[pl.BlockSpec((bm, d), lambda i, j: (i, 0), pipeline_mode=pl.Buffered(1)),
                  pl.BlockSpec((d, bf), lambda i, j: (0, j)),
                  pl.BlockSpec((d, bf), lambda i, j: (0, j))],
        out_specs=pl.BlockSpec((bm, bf), lambda i, j: (i, j)),
        out_shape=jax.ShapeDtypeStruct((m, d_ff), BF16),
        compiler_params=pltpu.CompilerParams(
            dimension_semantics=("arbitrary", "arbitrary"),
            vmem_limit_bytes=_vmem_limit(blk + bm * 2 * bf * 4)),
        name="swiglu",
    )(h, w_gate, w_up)


CONV_CW = 128
CONV_R = 64


def _conv_kernel(xv_ref, xg_ref, hv_ref, hg_ref, w_ref, b_ref, lg_ref, lb_ref, o_ref,
                 hext_ref, sh_ref, y_ref, *, t_rows, width):
    i = pl.program_id(0)
    hext_ref[CONV_HALO:, :] = xv_ref[...] * _sigmoid(xg_ref[...])
    halo = hv_ref[...] * _sigmoid(hg_ref[...])
    hext_ref[:CONV_HALO, :] = jnp.where(i > 0, halo, 0.0)

    first = CONV_HALO - (CONV_K - 1)
    n_sh = t_rows + CONV_HALO - SUBLANES
    for c in range(width // CONV_CW):
        cs = slice(c * CONV_CW, (c + 1) * CONV_CW)
        for b in range(1, SUBLANES):
            sh_ref[b - 1] = hext_ref[pl.ds(b, n_sh), cs]

        def row_tile(r, carry, cs=cs):
            r0 = pl.multiple_of(r * CONV_R, CONV_R)
            acc = jnp.zeros((CONV_R, CONV_CW), F32)
            for k in range(CONV_K):
                a, b = divmod(first + k, SUBLANES)
                if b == 0:
                    src = hext_ref[pl.ds(r0 + SUBLANES * a, CONV_R), cs]
                else:
                    src = sh_ref[b - 1, pl.ds(r0 + SUBLANES * a, CONV_R), :]
                acc = acc + w_ref[k:k + 1, cs] * src
            y_ref[pl.ds(r0, CONV_R), cs] = acc + b_ref[:, cs]
            return carry

        lax.fori_loop(0, t_rows // CONV_R, row_tile, 0)

    y = y_ref[...]
    mu = jnp.mean(y, axis=-1, keepdims=True)
    yc = y - mu
    var = jnp.mean(yc * yc, axis=-1, keepdims=True)
    z = yc * lax.rsqrt(var + EPS) * lg_ref[...] + lb_ref[...]
    o_ref[...] = (z * _sigmoid(z)).astype(o_ref.dtype)


def _conformer_conv(proj, conv_w, conv_b, ln_g, ln_b, width, t_rows=256):
    s = proj.shape[0]
    hb = t_rows // CONV_HALO
    row = lambda a: a.reshape(1, width)
    halo_map = lambda c: (lambda i: (jnp.maximum(i * hb - 1, 0), c))
    blk = (2 * t_rows + 2 * CONV_HALO + 40) * width * 4 + t_rows * width * 2
    scratch = ((t_rows + CONV_HALO) * width + 7 * (t_rows + CONV_HALO) * CONV_CW + t_rows * width) * 4
    return pl.pallas_call(
        functools.partial(_conv_kernel, t_rows=t_rows, width=width),
        grid=(s // t_rows,),
        in_specs=[pl.BlockSpec((t_rows, width), lambda i: (i, 0)),
                  pl.BlockSpec((t_rows, width), lambda i: (i, 1)),
                  pl.BlockSpec((CONV_HALO, width), halo_map(0)),
                  pl.BlockSpec((CONV_HALO, width), halo_map(1)),
                  pl.BlockSpec((CONV_K, width), lambda i: (0, 0)),
                  pl.BlockSpec((1, width), lambda i: (0, 0)),
                  pl.BlockSpec((1, width), lambda i: (0, 0)),
                  pl.BlockSpec((1, width), lambda i: (0, 0))],
        out_specs=pl.BlockSpec((t_rows, width), lambda i: (i, 0)),
        out_shape=jax.ShapeDtypeStruct((s, width), BF16),
        scratch_shapes=[pltpu.VMEM((t_rows + CONV_HALO, width), F32),
                        pltpu.VMEM((SUBLANES - 1, t_rows + CONV_HALO - SUBLANES, CONV_CW), F32),
                        pltpu.VMEM((t_rows, width), F32)],
        compiler_params=pltpu.CompilerParams(
            dimension_semantics=("arbitrary",),
            vmem_limit_bytes=_vmem_limit(blk + scratch)),
        name="conformer_conv",
    )(proj, proj, proj, proj, conv_w, row(conv_b), row(ln_g), row(ln_b))


HG_MXU_LEVELS = (3, 4)


def _decay_sum_matrix():
    c = CHUNK
    m = np.zeros((1 + len(HG_MXU_LEVELS), c, c), np.float32)
    for t in range(c):
        m[0, t, :t + 1] = 1.0
        for i, l in enumerate(HG_MXU_LEVELS):
            blk = c >> l
            mid = t - t % blk + blk // 2
            if t >= mid:
                m[1 + i, t, mid:t + 1] = 1.0
            else:
                m[1 + i, t, t + 1:mid] = 1.0
    m = m.reshape(-1, c)
    return np.concatenate([m, m], axis=1)


def _hgrn_kernel(q_ref, f_ref, v_ref, g_ref, lbl_ref, ng_ref, m_ref, o_ref, st_ref, *, tb, layer):
    t_blk = pl.program_id(1)

    @pl.when(t_blk == 0)
    def _():
        st_ref[...] = jnp.zeros_like(st_ref)

    c = CHUNK
    n_chunks = tb // c
    w = HG_PAIR * HG_DK
    heads = [slice(h * HG_DK, (h + 1) * HG_DK) for h in range(HG_PAIR)]
    lbl = lbl_ref[...]
    e = jnp.exp(lbl - jnp.max(lbl, axis=0, keepdims=True))
    lb = jnp.sum(e[:layer + 1], axis=0, keepdims=True) / jnp.sum(e, axis=0, keepdims=True)
    ng = ng_ref[...]
    m2 = m_ref[...]

    row = lax.broadcasted_iota(jnp.int32, (c, w), 0)
    upper_row = {l: (row % (c >> l)) >= (c >> (l + 1)) for l in range(3, N_LEVELS)}
    t_i = lax.broadcasted_iota(jnp.int32, (c, HG_PAIR * c), 0)
    s_i = lax.broadcasted_iota(jnp.int32, (c, HG_PAIR * c), 1) % c
    valid = []
    for l in range(N_LEVELS):
        blk = c >> l
        valid.append((t_i // blk == s_i // blk) & (t_i % blk >= blk // 2) & (s_i % blk < blk // 2))

    def blockdiag(a, zero):
        return jnp.concatenate([jnp.concatenate([a[:, heads[0]], zero], axis=1),
                                jnp.concatenate([zero, a[:, heads[1]]], axis=1)], axis=0)

    def per_head(fn):
        return jnp.concatenate([fn(hs) for hs in heads], axis=1)

    zero_c = jnp.zeros((c, HG_DK), BF16)
    zero_s = jnp.zeros((HG_DV, HG_DK), BF16)
    nt_dims = (((1,), (1,)), ((), ()))
    tn_dims = (((0,), (0,)), ((), ()))

    def level_operand(l, qf, kk, f, b_cum, ex):
        blk = c >> l
        half = blk // 2
        if blk >= 2 * SUBLANES:
            ref = jnp.concatenate([jnp.broadcast_to(b_cum[s0 + half - 1:s0 + half, :], (blk, w))
                                   for s0 in range(0, c, blk)], axis=0)
            sel = jnp.concatenate([p[s0 + o:s0 + o + half, :] for s0 in range(0, c, blk)
                                   for p, o in ((kk, 0), (qf, half))], axis=0)
            y = sel * jnp.exp(-jnp.abs(b_cum - ref))
        elif l in HG_MXU_LEVELS:
            i = HG_MXU_LEVELS.index(l)
            y = jnp.where(upper_row[l], qf, kk) * jnp.exp(ex[(1 + i) * c:(2 + i) * c])
        else:
            y = jnp.where(upper_row[l], qf * f, kk)
        return y.astype(BF16)

    st = [st_ref[h] for h in range(HG_PAIR)]
    gates = []
    for ci in range(n_chunks):
        rs = slice(ci * c, (ci + 1) * c)
        q = q_ref[rs, :]
        qf = q * _sigmoid(q)
        f = lb + (1.0 - lb) * _sigmoid(f_ref[rs, :])
        log_f = jnp.log(f)
        p0 = log_f.astype(BF16)
        p1 = (log_f - p0.astype(F32)).astype(BF16)
        gates.append((rs, qf, 1.0 - f, f, jnp.concatenate([p0, p1], axis=0)))
    all_fronts = []
    for cp in range(0, n_chunks, 2):
        fronts = []
        for rs, qf, kk, f, parts in gates[cp:cp + 2]:
            ex = jnp.dot(m2, parts, preferred_element_type=F32)
            fronts.append((rs, qf, kk, f, ex[:c], ex))
        all_fronts.append(fronts)

    raw = []
    updates = []
    for fronts in all_fronts:
        for l in range(N_LEVELS):
            ys = [level_operand(l, qf, kk, f, b_cum, ex) for (_, qf, kk, f, b_cum, ex) in fronts]
            raw.append(lax.dot_general(jnp.concatenate(ys, axis=0),
                                       jnp.concatenate([blockdiag(y, zero_c) for y in ys], axis=0),
                                       nt_dims, preferred_element_type=F32))
        for rs, qf, kk, f, b_cum, ex in fronts:
            b_last = b_cum[c - 1:c, :]
            ke = (kk * jnp.exp(b_last - b_cum)).astype(BF16)
            updates.append(lax.dot_general(v_ref[rs, :].astype(BF16), ke, tn_dims,
                                           preferred_element_type=F32))
    chunks = []
    for p, fronts in enumerate(all_fronts):
        for j, front in enumerate(fronts):
            scores = None
            for l in range(N_LEVELS):
                s_l = raw[p * N_LEVELS + l][j * c:(j + 1) * c, j * HG_PAIR * c:(j + 1) * HG_PAIR * c]
                s_l = jnp.where(valid[l], s_l, 0.0)
                scores = s_l if scores is None else scores + s_l
            chunks.append((front, scores.astype(BF16)))

    mids = []
    for ((rs, qf, kk, f, b_cum, ex), scores), ut in zip(chunks, updates):
        v = v_ref[rs, :]
        o = jnp.dot(scores, blockdiag(v.astype(BF16), zero_c), preferred_element_type=F32)
        qk = qf * kk
        o = o + per_head(lambda hs: jnp.broadcast_to(
            jnp.sum(qk[:, hs], axis=-1, keepdims=True), (c, HG_DK))) * v
        qe = (qf * jnp.exp(b_cum)).astype(BF16)
        mids.append((rs, o, ut, qe, jnp.exp(b_cum[c - 1:c, :])))

    st_in = []
    for rs, o, ut, qe, dec in mids:
        s0, s1 = (jnp.transpose(s).astype(BF16) for s in st)
        st_in.append(jnp.concatenate([jnp.concatenate([s0, zero_s], axis=1),
                                      jnp.concatenate([zero_s, s1], axis=1)], axis=0))
        for h, hs in enumerate(heads):
            st[h] = st[h] * dec[:, hs] + ut[hs, hs]
    for h in range(HG_PAIR):
        st_ref[h] = st[h]

    for (rs, o, ut, qe, dec), st_bd in zip(mids, st_in):
        g = g_ref[rs, :]
        o = o + jnp.dot(qe, st_bd, preferred_element_type=F32)
        o2 = o * o
        inv = per_head(lambda hs: jnp.broadcast_to(
            lax.rsqrt(jnp.mean(o2[:, hs], axis=-1, keepdims=True) + EPS), (c, HG_DV)))
        o_ref[rs, :] = (o * inv * ng * (g * _sigmoid(g))).astype(o_ref.dtype)


def _hgrn2(proj, lb_logits, norm_g, layer, col0, width, tb=512):
    s = proj.shape[0]
    n_heads = width // HG_DK
    bw = HG_PAIR * HG_DK
    assert n_heads % HG_PAIR == 0 and s % tb == 0 and tb % CHUNK == 0 and col0 % bw == 0
    sec = lambda k: (lambda hg, t: (t, (col0 + k * width) // bw + hg))
    depth1 = lb_logits.shape[0]
    m2 = jnp.asarray(_decay_sum_matrix(), BF16)
    blk = 4 * tb * bw * 4 + tb * bw * 2 + m2.size * 2
    return pl.pallas_call(
        functools.partial(_hgrn_kernel, tb=tb, layer=layer),
        grid=(n_heads // HG_PAIR, s // tb),
        in_specs=[pl.BlockSpec((tb, bw), sec(0)),
                  pl.BlockSpec((tb, bw), sec(1)),
                  pl.BlockSpec((tb, bw), sec(2)),
                  pl.BlockSpec((tb, bw), sec(3)),
                  pl.BlockSpec((depth1, bw), lambda hg, t: (0, hg)),
                  pl.BlockSpec((1, bw), lambda hg, t: (0, hg)),
                  pl.BlockSpec(m2.shape, lambda hg, t: (0, 0))],
        out_specs=pl.BlockSpec((tb, bw), lambda hg, t: (t, hg)),
        out_shape=jax.ShapeDtypeStruct((s, width), BF16),
        scratch_shapes=[pltpu.VMEM((HG_PAIR, HG_DV, HG_DK), F32)],
        compiler_params=pltpu.CompilerParams(
            dimension_semantics=("arbitrary", "arbitrary"),
            vmem_limit_bytes=_vmem_limit(blk + 8 * 1024 * 1024)),
        name="hgrn2",
    )(proj, proj, proj, proj, lb_logits, norm_g.reshape(1, width), m2)


def kernel(x, attn_norm_g, w_in, conv_w, conv_b, conv_ln_g, conv_ln_b, hg_lb_logits, hg_norm_g,
           w_out, ffn_norm_g, w_gate, w_up, w_down, final_norm_g):
    bsz, seq, d_model = x.shape
    assert bsz == 1
    depth = w_in.shape[0]
    conv_width = conv_w.shape[2]
    hg_width = hg_norm_g.shape[1]
    d_ff = w_gate.shape[2]
    bf = 256
    assert d_ff % bf == 0

    xs = x.reshape(seq, d_model)
    for l in range(depth):
        w_out_b = w_out[l].astype(BF16)
        w_down_b = w_down[l].astype(BF16)

        h = _rmsnorm(xs, attn_norm_g[l], BF16)
        proj = _matmul([h], w_in[l], None, F32, bm=2048, bn=512, lhs_bufs=1)
        a_out = _conformer_conv(proj, conv_w[l], conv_b[l], conv_ln_g[l], conv_ln_b[l], conv_width)
        b_out = _hgrn2(proj, hg_lb_logits, hg_norm_g[l], l, 2 * conv_width, hg_width)
        xs = _matmul([a_out, b_out], w_out_b, xs, F32, bm=1024, bn=1024)
        h = _rmsnorm(xs, ffn_norm_g[l], BF16)
        u = _swiglu(h, w_gate[l], w_up[l], bm=2048, bf=bf)
        xs = _matmul([u], w_down_b, xs, F32, bm=512, bn=512)
    out = _rmsnorm(xs, final_norm_g, F32)
    return out.reshape(bsz, seq, d_model)
```

```python
import functools

import numpy as np
import jax
import jax.numpy as jnp
from jax import lax
from jax.experimental import pallas as pl
from jax.experimental.pallas import tpu as pltpu

F32 = jnp.float32
BF16 = jnp.bfloat16
EPS = 1e-6

LANES = 128
SUBLANES = 8
VMEM_BYTES = 64 * 1024 * 1024
VMEM_HEADROOM = 6 * 1024 * 1024

HG_DK = 128
HG_DV = 128
HG_PAIR = 2
CHUNK = 64
N_LEVELS = 6
CONV_K = 31
CONV_HALO = 32


def _vmem_limit(block_bytes):
    return int(min(VMEM_BYTES - 2 * 1024 * 1024, 2 * block_bytes + VMEM_HEADROOM))


def _sigmoid(x):
    return 1.0 / (1.0 + jnp.exp(-x))


def _rmsnorm_kernel(x_ref, g_ref, o_ref):
    x = x_ref[...]
    ms = jnp.mean(x * x, axis=-1, keepdims=True)
    o_ref[...] = (x * lax.rsqrt(ms + EPS) * g_ref[...]).astype(o_ref.dtype)


def _rmsnorm(x, g, out_dtype, bm=256):
    m, d = x.shape
    return pl.pallas_call(
        _rmsnorm_kernel,
        grid=(m // bm,),
        in_specs=[pl.BlockSpec((bm, d), lambda i: (i, 0)),
                  pl.BlockSpec((1, d), lambda i: (0, 0))],
        out_specs=pl.BlockSpec((bm, d), lambda i: (i, 0)),
        out_shape=jax.ShapeDtypeStruct((m, d), out_dtype),
        compiler_params=pltpu.CompilerParams(
            dimension_semantics=("arbitrary",),
            vmem_limit_bytes=_vmem_limit(bm * d * 8)),
        name="rmsnorm",
    )(x, g.reshape(1, d))


def _matmul_kernel(*refs, n_lhs, has_res, nk, n_side):
    lhs = refs[:n_lhs]
    ws = refs[n_lhs:2 * n_lhs]
    res_ref = refs[2 * n_lhs] if has_res else None
    n_in = 2 * n_lhs + has_res
    o_ref = refs[n_in + n_side]
    for s_in, s_out in zip(refs[n_in:n_in + n_side], refs[n_in + n_side + 1:]):
        s_out[...] = s_in[...].astype(s_out.dtype)
    acc = None
    for a_ref, w_ref in zip(lhs, ws):
        d = jnp.dot(a_ref[...], w_ref[...].astype(BF16), preferred_element_type=F32)
        acc = d if acc is None else acc + d
    if nk == 1:
        if has_res:
            acc = acc + res_ref[...]
        o_ref[...] = acc.astype(o_ref.dtype)
    else:
        k = pl.program_id(2)

        @pl.when(k == 0)
        def _():
            o_ref[...] = acc + res_ref[...] if has_res else acc

        @pl.when(k > 0)
        def _():
            o_ref[...] += acc


def _matmul(lhs_list, w, res, out_dtype, bm, bn, bk=None, lhs_bufs=2, side_casts=(), side_rows=256):
    m, kdim = lhs_list[0].shape
    n = w.shape[1]
    bk = kdim if bk is None else bk
    nk = kdim // bk
    n_lhs = len(lhs_list)
    assert m % bm == 0 and n % bn == 0 and kdim % bk == 0 and w.shape[0] == n_lhs * kdim
    assert nk == 1 or out_dtype == F32
    w_bytes = jnp.dtype(w.dtype).itemsize
    w_spec = lambda p: pl.BlockSpec((bk, bn), lambda i, j, k: (p * nk + k, j))
    assert lhs_bufs == 2 or nk == 1
    in_specs = ([pl.BlockSpec((bm, bk), lambda i, j, k: (i, k), pipeline_mode=pl.Buffered(lhs_bufs))] * n_lhs
                + [w_spec(p) for p in range(n_lhs)])
    args = list(lhs_list) + [w] * n_lhs
    w_tmp = 1 if w.dtype != BF16 else 0
    blk = (n_lhs * (bm * bk * lhs_bufs + bk * bn * (w_bytes + w_tmp))
           + bm * bn * jnp.dtype(out_dtype).itemsize)
    if res is not None:
        in_specs.append(pl.BlockSpec((bm, bn), lambda i, j, k: (i, j)))
        args.append(res)
        blk += bm * bn * 4
    nj = n // bn
    out_specs = [pl.BlockSpec((bm, bn), lambda i, j, k: (i, j))]
    out_shape = [jax.ShapeDtypeStruct((m, n), out_dtype)]
    for a in side_casts:
        rows, cols = a.shape
        n_blk = rows // side_rows
        assert nk == 1 and rows % side_rows == 0 and n_blk <= (m // bm) * nj
        side_map = lambda i, j, k, n_blk=n_blk: (jnp.minimum(i * nj + j, n_blk - 1), 0)
        in_specs.append(pl.BlockSpec((side_rows, cols), side_map))
        out_specs.append(pl.BlockSpec((side_rows, cols), side_map))
        out_shape.append(jax.ShapeDtypeStruct((rows, cols), BF16))
        args.append(a)
        blk += side_rows * cols * 6
    outs = pl.pallas_call(
        functools.partial(_matmul_kernel, n_lhs=n_lhs, has_res=res is not None, nk=nk,
                          n_side=len(side_casts)),
        grid=(m // bm, nj, nk),
        in_specs=in_specs,
        out_specs=out_specs,
        out_shape=out_shape,
        compiler_params=pltpu.CompilerParams(
            dimension_semantics=("arbitrary", "arbitrary", "arbitrary"),
            vmem_limit_bytes=_vmem_limit(blk + bm * bn * 4)),
        name="matmul",
    )(*args)
    return outs if side_casts else outs[0]


SWIGLU_ROW_SPLITS = 4


def _swiglu_kernel(h_ref, wg_ref, wu_ref, u_ref):
    wg = wg_ref[...].astype(BF16)
    wu = wu_ref[...].astype(BF16)
    rows = h_ref.shape[0] // SWIGLU_ROW_SPLITS
    for r in range(SWIGLU_ROW_SPLITS):
        rs = slice(r * rows, (r + 1) * rows)
        h = h_ref[rs, :]
        g = jnp.dot(h, wg, preferred_element_type=F32)
        u = jnp.dot(h, wu, preferred_element_type=F32)
        u_ref[rs, :] = (g * _sigmoid(g) * u).astype(u_ref.dtype)


def _swiglu(h, w_gate, w_up, bm, bf):
    m, d = h.shape
    d_ff = w_gate.shape[1]
    assert m % bm == 0 and d_ff % bf == 0
    w_bytes = jnp.dtype(w_gate.dtype).itemsize
    blk = bm * d + 2 * d * bf * (w_bytes + 1) + bm * bf * 2
    return pl.pallas_call(
        _swiglu_kernel,
        grid=(m // bm, d_ff // bf),
        in_specs=[pl.BlockSpec((bm, d), lambda i, j: (i, 0), pipeline_mode=pl.Buffered(1)),
                  pl.BlockSpec((d, bf), lambda i, j: (0, j)),
                  pl.BlockSpec((d, bf), lambda i, j: (0, j))],
        out_specs=pl.BlockSpec((bm, bf), lambda i, j: (i, j)),
        out_shape=jax.ShapeDtypeStruct((m, d_ff), BF16),
        compiler_params=pltpu.CompilerParams(
            dimension_semantics=("arbitrary", "arbitrary"),
            vmem_limit_bytes=_vmem_limit(blk + bm * 2 * bf * 4)),
        name="swiglu",
    )(h, w_gate, w_up)


CONV_CW = 128
CONV_R = 64


def _conv_kernel(xv_ref, xg_ref, hv_ref, hg_ref, w_ref, b_ref, lg_ref, lb_ref, o_ref,
                 hext_ref, sh_ref, y_ref, *, t_rows, width):
    i = pl.program_id(0)
    hext_ref[CONV_HALO:, :] = xv_ref[...] * _sigmoid(xg_ref[...])
    halo = hv_ref[...] * _sigmoid(hg_ref[...])
    hext_ref[:CONV_HALO, :] = jnp.where(i > 0, halo, 0.0)

    first = CONV_HALO - (CONV_K - 1)
    n_sh = t_rows + CONV_HALO - SUBLANES
    for c in range(width // CONV_CW):
        cs = slice(c * CONV_CW, (c + 1) * CONV_CW)
        for b in range(1, SUBLANES):
            sh_ref[b - 1] = hext_ref[pl.ds(b, n_sh), cs]

        def row_tile(r, carry, cs=cs):
            r0 = pl.multiple_of(r * CONV_R, CONV_R)
            acc = jnp.zeros((CONV_R, CONV_CW), F32)
            for k in range(CONV_K):
                a, b = divmod(first + k, SUBLANES)
                if b == 0:
                    src = hext_ref[pl.ds(r0 + SUBLANES * a, CONV_R), cs]
                else:
                    src = sh_ref[b - 1, pl.ds(r0 + SUBLANES * a, CONV_R), :]
                acc = acc + w_ref[k:k + 1, cs] * src
            y_ref[pl.ds(r0, CONV_R), cs] = acc + b_ref[:, cs]
            return carry

        lax.fori_loop(0, t_rows // CONV_R, row_tile, 0)

    y = y_ref[...]
    mu = jnp.mean(y, axis=-1, keepdims=True)
    yc = y - mu
    var = jnp.mean(yc * yc, axis=-1, keepdims=True)
    z = yc * lax.rsqrt(var + EPS) * lg_ref[...] + lb_ref[...]
    o_ref[...] = (z * _sigmoid(z)).astype(o_ref.dtype)


def _conformer_conv(proj, conv_w, conv_b, ln_g, ln_b, width, t_rows=256):
    s = proj.shape[0]
    hb = t_rows // CONV_HALO
    row = lambda a: a.reshape(1, width)
    halo_map = lambda c: (lambda i: (jnp.maximum(i * hb - 1, 0), c))
    blk = (2 * t_rows + 2 * CONV_HALO + 40) * width * 4 + t_rows * width * 2
    scratch = ((t_rows + CONV_HALO) * width + 7 * (t_rows + CONV_HALO) * CONV_CW + t_rows * width) * 4
    return pl.pallas_call(
        functools.partial(_conv_kernel, t_rows=t_rows, width=width),
        grid=(s // t_rows,),
        in_specs=[pl.BlockSpec((t_rows, width), lambda i: (i, 0)),
                  pl.BlockSpec((t_rows, width), lambda i: (i, 1)),
                  pl.BlockSpec((CONV_HALO, width), halo_map(0)),
                  pl.BlockSpec((CONV_HALO, width), halo_map(1)),
                  pl.BlockSpec((CONV_K, width), lambda i: (0, 0)),
                  pl.BlockSpec((1, width), lambda i: (0, 0)),
                  pl.BlockSpec((1, width), lambda i: (0, 0)),
                  pl.BlockSpec((1, width), lambda i: (0, 0))],
        out_specs=pl.BlockSpec((t_rows, width), lambda i: (i, 0)),
        out_shape=jax.ShapeDtypeStruct((s, width), BF16),
        scratch_shapes=[pltpu.VMEM((t_rows + CONV_HALO, width), F32),
                        pltpu.VMEM((SUBLANES - 1, t_rows + CONV_HALO - SUBLANES, CONV_CW), F32),
                        pltpu.VMEM((t_rows, width), F32)],
        compiler_params=pltpu.CompilerParams(
            dimension_semantics=("arbitrary",),
            vmem_limit_bytes=_vmem_limit(blk + scratch)),
        name="conformer_conv",
    )(proj, proj, proj, proj, conv_w, row(conv_b), row(ln_g), row(ln_b))


HG_MXU_LEVELS = (3, 4)


def _decay_sum_matrix():
    c = CHUNK
    m = np.zeros((1 + len(HG_MXU_LEVELS), c, c), np.float32)
    for t in range(c):
        m[0, t, :t + 1] = 1.0
        for i, l in enumerate(HG_MXU_LEVELS):
            blk = c >> l
            mid = t - t % blk + blk // 2
            if t >= mid:
                m[1 + i, t, mid:t + 1] = 1.0
            else:
                m[1 + i, t, t + 1:mid] = 1.0
    m = m.reshape(-1, c)
    return np.concatenate([m, m], axis=1)


def _hgrn_kernel(q_ref, f_ref, v_ref, g_ref, lbl_ref, ng_ref, m_ref, o_ref, st_ref, *, tb, layer):
    t_blk = pl.program_id(1)

    @pl.when(t_blk == 0)
    def _():
        st_ref[...] = jnp.zeros_like(st_ref)

    c = CHUNK
    n_chunks = tb // c
    w = HG_PAIR * HG_DK
    heads = [slice(h * HG_DK, (h + 1) * HG_DK) for h in range(HG_PAIR)]
    lbl = lbl_ref[...]
    e = jnp.exp(lbl - jnp.max(lbl, axis=0, keepdims=True))
    lb = jnp.sum(e[:layer + 1], axis=0, keepdims=True) / jnp.sum(e, axis=0, keepdims=True)
    ng = ng_ref[...]
    m2 = m_ref[...]

    row = lax.broadcasted_iota(jnp.int32, (c, w), 0)
    upper_row = {l: (row % (c >> l)) >= (c >> (l + 1)) for l in range(3, N_LEVELS)}
    t_i = lax.broadcasted_iota(jnp.int32, (c, HG_PAIR * c), 0)
    s_i = lax.broadcasted_iota(jnp.int32, (c, HG_PAIR * c), 1) % c
    valid = []
    for l in range(N_LEVELS):
        blk = c >> l
        valid.append((t_i // blk == s_i // blk) & (t_i % blk >= blk // 2) & (s_i % blk < blk // 2))

    def blockdiag(a, zero):
        return jnp.concatenate([jnp.concatenate([a[:, heads[0]], zero], axis=1),
                                jnp.concatenate([zero, a[:, heads[1]]], axis=1)], axis=0)

    def per_head(fn):
        return jnp.concatenate([fn(hs) for hs in heads], axis=1)

    zero_c = jnp.zeros((c, HG_DK), BF16)
    zero_s = jnp.zeros((HG_DV, HG_DK), BF16)
    nt_dims = (((1,), (1,)), ((), ()))
    tn_dims = (((0,), (0,)), ((), ()))

    def level_operand(l, qf, kk, f, b_cum, ex):
        blk = c >> l
        half = blk // 2
        if blk >= 2 * SUBLANES:
            ref = jnp.concatenate([jnp.broadcast_to(b_cum[s0 + half - 1:s0 + half, :], (blk, w))
                                   for s0 in range(0, c, blk)], axis=0)
            sel = jnp.concatenate([p[s0 + o:s0 + o + half, :] for s0 in range(0, c, blk)
                                   for p, o in ((kk, 0), (qf, half))], axis=0)
            y = sel * jnp.exp(-jnp.abs(b_cum - ref))
        elif l in HG_MXU_LEVELS:
            i = HG_MXU_LEVELS.index(l)
            y = jnp.where(upper_row[l], qf, kk) * jnp.exp(ex[(1 + i) * c:(2 + i) * c])
        else:
            y = jnp.where(upper_row[l], qf * f, kk)
        return y.astype(BF16)

    st = [st_ref[h] for h in range(HG_PAIR)]
    gates = []
    for ci in range(n_chunks):
        rs = slice(ci * c, (ci + 1) * c)
        q = q_ref[rs, :]
        qf = q * _sigmoid(q)
        f = lb + (1.0 - lb) * _sigmoid(f_ref[rs, :])
        log_f = jnp.log(f)
        p0 = log_f.astype(BF16)
        p1 = (log_f - p0.astype(F32)).astype(BF16)
        gates.append((rs, qf, 1.0 - f, f, jnp.concatenate([p0, p1], axis=0)))
    all_fronts = []
    for cp in range(0, n_chunks, 2):
        fronts = []
        for rs, qf, kk, f, parts in gates[cp:cp + 2]:
            ex = jnp.dot(m2, parts, preferred_element_type=F32)
            fronts.append((rs, qf, kk, f, ex[:c], ex))
        all_fronts.append(fronts)

    raw = []
    updates = []
    for fronts in all_fronts:
        for l in range(N_LEVELS):
            ys = [level_operand(l, qf, kk, f, b_cum, ex) for (_, qf, kk, f, b_cum, ex) in fronts]
            raw.append(lax.dot_general(jnp.concatenate(ys, axis=0),
                                       jnp.concatenate([blockdiag(y, zero_c) for y in ys], axis=0),
                                       nt_dims, preferred_element_type=F32))
        for rs, qf, kk, f, b_cum, ex in fronts:
            b_last = b_cum[c - 1:c, :]
            ke = (kk * jnp.exp(b_last - b_cum)).astype(BF16)
            updates.append(lax.dot_general(v_ref[rs, :].astype(BF16), ke, tn_dims,
                                           preferred_element_type=F32))
    chunks = []
    for p, fronts in enumerate(all_fronts):
        for j, front in enumerate(fronts):
            scores = None
            for l in range(N_LEVELS):
                s_l = raw[p * N_LEVELS + l][j * c:(j + 1) * c, j * HG_PAIR * c:(j + 1) * HG_PAIR * c]
                s_l = jnp.where(valid[l], s_l, 0.0)
                scores = s_l if scores is None else scores + s_l
            chunks.append((front, scores.astype(BF16)))

    mids = []
    for ((rs, qf, kk, f, b_cum, ex), scores), ut in zip(chunks, updates):
        v = v_ref[rs, :]
        o = jnp.dot(scores, blockdiag(v.astype(BF16), zero_c), preferred_element_type=F32)
        qk = qf * kk
        o = o + per_head(lambda hs: jnp.broadcast_to(
            jnp.sum(qk[:, hs], axis=-1, keepdims=True), (c, HG_DK))) * v
        qe = (qf * jnp.exp(b_cum)).astype(BF16)
        mids.append((rs, o, ut, qe, jnp.exp(b_cum[c - 1:c, :])))

    st_in = []
    for rs, o, ut, qe, dec in mids:
        s0, s1 = (jnp.transpose(s).astype(BF16) for s in st)
        st_in.append(jnp.concatenate([jnp.concatenate([s0, zero_s], axis=1),
                                      jnp.concatenate([zero_s, s1], axis=1)], axis=0))
        for h, hs in enumerate(heads):
            st[h] = st[h] * dec[:, hs] + ut[hs, hs]
    for h in range(HG_PAIR):
        st_ref[h] = st[h]

    for (rs, o, ut, qe, dec), st_bd in zip(mids, st_in):
        g = g_ref[rs, :]
        o = o + jnp.dot(qe, st_bd, preferred_element_type=F32)
        o2 = o * o
        inv = per_head(lambda hs: jnp.broadcast_to(
            lax.rsqrt(jnp.mean(o2[:, hs], axis=-1, keepdims=True) + EPS), (c, HG_DV)))
        o_ref[rs, :] = (o * inv * ng * (g * _sigmoid(g))).astype(o_ref.dtype)


def _hgrn2(proj, lb_logits, norm_g, layer, col0, width, tb=512):
    s = proj.shape[0]
    n_heads = width // HG_DK
    bw = HG_PAIR * HG_DK
    assert n_heads % HG_PAIR == 0 and s % tb == 0 and tb % CHUNK == 0 and col0 % bw == 0
    sec = lambda k: (lambda hg, t: (t, (col0 + k * width) // bw + hg))
    depth1 = lb_logits.shape[0]
    m2 = jnp.asarray(_decay_sum_matrix(), BF16)
    blk = 4 * tb * bw * 4 + tb * bw * 2 + m2.size * 2
    return pl.pallas_call(
        functools.partial(_hgrn_kernel, tb=tb, layer=layer),
        grid=(n_heads // HG_PAIR, s // tb),
        in_specs=[pl.BlockSpec((tb, bw), sec(0)),
                  pl.BlockSpec((tb, bw), sec(1)),
                  pl.BlockSpec((tb, bw), sec(2)),
                  pl.BlockSpec((tb, bw), sec(3)),
                  pl.BlockSpec((depth1, bw), lambda hg, t: (0, hg)),
                  pl.BlockSpec((1, bw), lambda hg, t: (0, hg)),
                  pl.BlockSpec(m2.shape, lambda hg, t: (0, 0))],
        out_specs=pl.BlockSpec((tb, bw), lambda hg, t: (t, hg)),
        out_shape=jax.ShapeDtypeStruct((s, width), BF16),
        scratch_shapes=[pltpu.VMEM((HG_PAIR, HG_DV, HG_DK), F32)],
        compiler_params=pltpu.CompilerParams(
            dimension_semantics=("arbitrary", "arbitrary"),
            vmem_limit_bytes=_vmem_limit(blk + 8 * 1024 * 1024)),
        name="hgrn2",
    )(proj, proj, proj, proj, lb_logits, norm_g.reshape(1, width), m2)


def kernel(x, attn_norm_g, w_in, conv_w, conv_b, conv_ln_g, conv_ln_b, hg_lb_logits, hg_norm_g,
           w_out, ffn_norm_g, w_gate, w_up, w_down, final_norm_g):
    bsz, seq, d_model = x.shape
    assert bsz == 1
    depth = w_in.shape[0]
    conv_width = conv_w.shape[2]
    hg_width = hg_norm_g.shape[1]
    d_ff = w_gate.shape[2]
    bf = 256
    assert d_ff % bf == 0

    xs = x.reshape(seq, d_model)
    for l in range(depth):
        h = _rmsnorm(xs, attn_norm_g[l], BF16)
        proj, w_out_b, w_down_b = _matmul([h], w_in[l], None, F32, bm=2048, bn=512, lhs_bufs=1,
                                          side_casts=(w_out[l], w_down[l]), side_rows=128)
        a_out = _conformer_conv(proj, conv_w[l], conv_b[l], conv_ln_g[l], conv_ln_b[l], conv_width)
        b_out = _hgrn2(proj, hg_lb_logits, hg_norm_g[l], l, 2 * conv_width, hg_width)
        xs = _matmul([a_out, b_out], w_out_b, xs, F32, bm=1024, bn=1024)
        h = _rmsnorm(xs, ffn_norm_g[l], BF16)
        u = _swiglu(h, w_gate[l], w_up[l], bm=2048, bf=bf)
        xs = _matmul([u], w_down_b, xs, F32, bm=512, bn=512)
    out = _rmsnorm(xs, final_norm_g, F32)
    return out.reshape(bsz, seq, d_model)
```

```python
import functools

import numpy as np
import jax
import jax.numpy as jnp
from jax import lax
from jax.experimental import pallas as pl
from jax.experimental.pallas import tpu as pltpu

F32 = jnp.float32
BF16 = jnp.bfloat16
EPS = 1e-6

LANES = 128
SUBLANES = 8
VMEM_BYTES = 64 * 1024 * 1024
VMEM_HEADROOM = 6 * 1024 * 1024

HG_DK = 128
HG_DV = 128
HG_PAIR = 2
CHUNK = 64
N_LEVELS = 6
CONV_K = 31
CONV_HALO = 32


def _vmem_limit(block_bytes):
    return int(min(VMEM_BYTES - 2 * 1024 * 1024, 2 * block_bytes + VMEM_HEADROOM))


def _sigmoid(x):
    return 1.0 / (1.0 + jnp.exp(-x))


def _silu(x):
    hx = 0.5 * x
    return hx + hx * jnp.tanh(hx)


def _rmsnorm_kernel(x_ref, g_ref, o_ref):
    x = x_ref[...]
    ms = jnp.mean(x * x, axis=-1, keepdims=True)
    o_ref[...] = (x * lax.rsqrt(ms + EPS) * g_ref[...]).astype(o_ref.dtype)


def _rmsnorm(x, g, out_dtype, bm=256):
    m, d = x.shape
    return pl.pallas_call(
        _rmsnorm_kernel,
        grid=(m // bm,),
        in_specs=[pl.BlockSpec((bm, d), lambda i: (i, 0)),
                  pl.BlockSpec((1, d), lambda i: (0, 0))],
        out_specs=pl.BlockSpec((bm, d), lambda i: (i, 0)),
        out_shape=jax.ShapeDtypeStruct((m, d), out_dtype),
        compiler_params=pltpu.CompilerParams(
            dimension_semantics=("arbitrary",),
            vmem_limit_bytes=_vmem_limit(bm * d * 8)),
        name="rmsnorm",
    )(x, g.reshape(1, d))


def _matmul_kernel(*refs, n_lhs, has_res, nk, n_side):
    lhs = refs[:n_lhs]
    ws = refs[n_lhs:2 * n_lhs]
    res_ref = refs[2 * n_lhs] if has_res else None
    n_in = 2 * n_lhs + has_res
    o_ref = refs[n_in + n_side]
    for s_in, s_out in zip(refs[n_in:n_in + n_side], refs[n_in + n_side + 1:]):
        s_out[...] = s_in[...].astype(s_out.dtype)
    acc = None
    for a_ref, w_ref in zip(lhs, ws):
        d = jnp.dot(a_ref[...], w_ref[...].astype(BF16), preferred_element_type=F32)
        acc = d if acc is None else acc + d
    if nk == 1:
        if has_res:
            acc = acc + res_ref[...]
        o_ref[...] = acc.astype(o_ref.dtype)
    else:
        k = pl.program_id(2)

        @pl.when(k == 0)
        def _():
            o_ref[...] = acc + res_ref[...] if has_res else acc

        @pl.when(k > 0)
        def _():
            o_ref[...] += acc


def _matmul(lhs_list, w, res, out_dtype, bm, bn, bk=None, lhs_bufs=2, side_casts=(), side_rows=256):
    m, kdim = lhs_list[0].shape
    n = w.shape[1]
    bk = kdim if bk is None else bk
    nk = kdim // bk
    n_lhs = len(lhs_list)
    assert m % bm == 0 and n % bn == 0 and kdim % bk == 0 and w.shape[0] == n_lhs * kdim
    assert nk == 1 or out_dtype == F32
    w_bytes = jnp.dtype(w.dtype).itemsize
    w_spec = lambda p: pl.BlockSpec((bk, bn), lambda i, j, k: (p * nk + k, j))
    assert lhs_bufs == 2 or nk == 1
    in_specs = ([pl.BlockSpec((bm, bk), lambda i, j, k: (i, k), pipeline_mode=pl.Buffered(lhs_bufs))] * n_lhs
                + [w_spec(p) for p in range(n_lhs)])
    args = list(lhs_list) + [w] * n_lhs
    w_tmp = 1 if w.dtype != BF16 else 0
    blk = (n_lhs * (bm * bk * lhs_bufs + bk * bn * (w_bytes + w_tmp))
           + bm * bn * jnp.dtype(out_dtype).itemsize)
    if res is not None:
        in_specs.append(pl.BlockSpec((bm, bn), lambda i, j, k: (i, j)))
        args.append(res)
        blk += bm * bn * 4
    nj = n // bn
    out_specs = [pl.BlockSpec((bm, bn), lambda i, j, k: (i, j))]
    out_shape = [jax.ShapeDtypeStruct((m, n), out_dtype)]
    for a in side_casts:
        rows, cols = a.shape
        n_blk = rows // side_rows
        assert nk == 1 and rows % side_rows == 0 and n_blk <= (m // bm) * nj
        side_map = lambda i, j, k, n_blk=n_blk: (jnp.minimum(i * nj + j, n_blk - 1), 0)
        in_specs.append(pl.BlockSpec((side_rows, cols), side_map))
        out_specs.append(pl.BlockSpec((side_rows, cols), side_map))
        out_shape.append(jax.ShapeDtypeStruct((rows, cols), BF16))
        args.append(a)
        blk += side_rows * cols * 6
    outs = pl.pallas_call(
        functools.partial(_matmul_kernel, n_lhs=n_lhs, has_res=res is not None, nk=nk,
                          n_side=len(side_casts)),
        grid=(m // bm, nj, nk),
        in_specs=in_specs,
        out_specs=out_specs,
        out_shape=out_shape,
        compiler_params=pltpu.CompilerParams(
            dimension_semantics=("arbitrary", "arbitrary", "arbitrary"),
            vmem_limit_bytes=_vmem_limit(blk + bm * bn * 4)),
        name="matmul",
    )(*args)
    return outs if side_casts else outs[0]


SWIGLU_ROW_SPLITS = 4


def _swiglu_kernel(h_ref, wg_ref, wu_ref, u_ref):
    wg = wg_ref[...].astype(BF16)
    wu = wu_ref[...].astype(BF16)
    rows = h_ref.shape[0] // SWIGLU_ROW_SPLITS
    for r in range(SWIGLU_ROW_SPLITS):
        rs = slice(r * rows, (r + 1) * rows)
        h = h_ref[rs, :]
        g = jnp.dot(h, wg, preferred_element_type=F32)
        u = jnp.dot(h, wu, preferred_element_type=F32)
        u_ref[rs, :] = (_silu(g) * u).astype(u_ref.dtype)


def _swiglu(h, w_gate, w_up, bm, bf):
    m, d = h.shape
    d_ff = w_gate.shape[1]
    assert m % bm == 0 and d_ff % bf == 0
    w_bytes = jnp.dtype(w_gate.dtype).itemsize
    blk = bm * d + 2 * d * bf * (w_bytes + 1) + bm * bf * 2
    return pl.pallas_call(
        _swiglu_kernel,
        grid=(m // bm, d_ff // bf),
        in_specs=[pl.BlockSpec((bm, d), lambda i, j: (i, 0), pipeline_mode=pl.Buffered(1)),
                  pl.BlockSpec((d, bf), lambda i, j: (0, j)),
                  pl.BlockSpec((d, bf), lambda i, j: (0, j))],
        out_specs=pl.BlockSpec((bm, bf), lambda i, j: (i, j)),
        out_shape=jax.ShapeDtypeStruct((m, d_ff), BF16),
        compiler_params=pltpu.CompilerParams(
            dimension_semantics=("arbitrary", "arbitrary"),
            vmem_limit_bytes=_vmem_limit(blk + bm * 2 * bf * 4)),
        name="swiglu",
    )(h, w_gate, w_up)


CONV_CW = 128
CONV_R = 64


def _conv_kernel(xv_ref, xg_ref, hv_ref, hg_ref, w_ref, b_ref, lg_ref, lb_ref, o_ref,
                 hext_ref, sh_ref, y_ref, *, t_rows, width):
    i = pl.program_id(0)
    hext_ref[CONV_HALO:, :] = xv_ref[...] * _sigmoid(xg_ref[...])
    halo = hv_ref[...] * _sigmoid(hg_ref[...])
    hext_ref[:CONV_HALO, :] = jnp.where(i > 0, halo, 0.0)

    first = CONV_HALO - (CONV_K - 1)
    n_sh = t_rows + CONV_HALO - SUBLANES
    for c in range(width // CONV_CW):
        cs = slice(c * CONV_CW, (c + 1) * CONV_CW)
        for b in range(1, SUBLANES):
            sh_ref[b - 1] = hext_ref[pl.ds(b, n_sh), cs]

        def row_tile(r, carry, cs=cs):
            r0 = pl.multiple_of(r * CONV_R, CONV_R)
            acc = jnp.zeros((CONV_R, CONV_CW), F32)
            for k in range(CONV_K):
                a, b = divmod(first + k, SUBLANES)
                if b == 0:
                    src = hext_ref[pl.ds(r0 + SUBLANES * a, CONV_R), cs]
                else:
                    src = sh_ref[b - 1, pl.ds(r0 + SUBLANES * a, CONV_R), :]
                acc = acc + w_ref[k:k + 1, cs] * src
            y_ref[pl.ds(r0, CONV_R), cs] = acc + b_ref[:, cs]
            return carry

        lax.fori_loop(0, t_rows // CONV_R, row_tile, 0)

    y = y_ref[...]
    mu = jnp.mean(y, axis=-1, keepdims=True)
    yc = y - mu
    var = jnp.mean(yc * yc, axis=-1, keepdims=True)
    z = yc * lax.rsqrt(var + EPS) * lg_ref[...] + lb_ref[...]
    o_ref[...] = (z * _sigmoid(z)).astype(o_ref.dtype)


def _conformer_conv(proj, conv_w, conv_b, ln_g, ln_b, width, t_rows=256):
    s = proj.shape[0]
    hb = t_rows // CONV_HALO
    row = lambda a: a.reshape(1, width)
    halo_map = lambda c: (lambda i: (jnp.maximum(i * hb - 1, 0), c))
    blk = (2 * t_rows + 2 * CONV_HALO + 40) * width * 4 + t_rows * width * 2
    scratch = ((t_rows + CONV_HALO) * width + 7 * (t_rows + CONV_HALO) * CONV_CW + t_rows * width) * 4
    return pl.pallas_call(
        functools.partial(_conv_kernel, t_rows=t_rows, width=width),
        grid=(s // t_rows,),
        in_specs=[pl.BlockSpec((t_rows, width), lambda i: (i, 0)),
                  pl.BlockSpec((t_rows, width), lambda i: (i, 1)),
                  pl.BlockSpec((CONV_HALO, width), halo_map(0)),
                  pl.BlockSpec((CONV_HALO, width), halo_map(1)),
                  pl.BlockSpec((CONV_K, width), lambda i: (0, 0)),
                  pl.BlockSpec((1, width), lambda i: (0, 0)),
                  pl.BlockSpec((1, width), lambda i: (0, 0)),
                  pl.BlockSpec((1, width), lambda i: (0, 0))],
        out_specs=pl.BlockSpec((t_rows, width), lambda i: (i, 0)),
        out_shape=jax.ShapeDtypeStruct((s, width), BF16),
        scratch_shapes=[pltpu.VMEM((t_rows + CONV_HALO, width), F32),
                        pltpu.VMEM((SUBLANES - 1, t_rows + CONV_HALO - SUBLANES, CONV_CW), F32),
                        pltpu.VMEM((t_rows, width), F32)],
        compiler_params=pltpu.CompilerParams(
            dimension_semantics=("arbitrary",),
            vmem_limit_bytes=_vmem_limit(blk + scratch)),
        name="conformer_conv",
    )(proj, proj, proj, proj, conv_w, row(conv_b), row(ln_g), row(ln_b))


HG_MXU_LEVELS = (3, 4)


def _decay_sum_matrix():
    c = CHUNK
    m = np.zeros((1 + len(HG_MXU_LEVELS), c, c), np.float32)
    for t in range(c):
        m[0, t, :t + 1] = 1.0
        for i, l in enumerate(HG_MXU_LEVELS):
            blk = c >> l
            mid = t - t % blk + blk // 2
            if t >= mid:
                m[1 + i, t, mid:t + 1] = 1.0
            else:
                m[1 + i, t, t + 1:mid] = 1.0
    m = m.reshape(-1, c)
    return np.concatenate([m, m], axis=1)


def _hgrn_kernel(q_ref, f_ref, v_ref, g_ref, lbl_ref, ng_ref, m_ref, o_ref, st_ref, *, tb, layer):
    t_blk = pl.program_id(1)

    @pl.when(t_blk == 0)
    def _():
        st_ref[...] = jnp.zeros_like(st_ref)

    c = CHUNK
    n_chunks = tb // c
    w = HG_PAIR * HG_DK
    heads = [slice(h * HG_DK, (h + 1) * HG_DK) for h in range(HG_PAIR)]
    lbl = lbl_ref[...]
    e = jnp.exp(lbl - jnp.max(lbl, axis=0, keepdims=True))
    lb = jnp.sum(e[:layer + 1], axis=0, keepdims=True) / jnp.sum(e, axis=0, keepdims=True)
    f_half = 0.5 * (1.0 - lb)
    f_mid = lb + f_half
    ng = ng_ref[...]
    m2 = m_ref[...]

    row = lax.broadcasted_iota(jnp.int32, (c, w), 0)
    upper_row = {l: (row % (c >> l)) >= (c >> (l + 1)) for l in range(3, N_LEVELS)}
    t_i = lax.broadcasted_iota(jnp.int32, (c, HG_PAIR * c), 0)
    s_i = lax.broadcasted_iota(jnp.int32, (c, HG_PAIR * c), 1) % c
    valid = []
    for l in range(N_LEVELS):
        blk = c >> l
        valid.append((t_i // blk == s_i // blk) & (t_i % blk >= blk // 2) & (s_i % blk < blk // 2))

    def blockdiag(a, zero):
        return jnp.concatenate([jnp.concatenate([a[:, heads[0]], zero], axis=1),
                                jnp.concatenate([zero, a[:, heads[1]]], axis=1)], axis=0)

    def per_head(fn):
        return jnp.concatenate([fn(hs) for hs in heads], axis=1)

    zero_c = jnp.zeros((c, HG_DK), BF16)
    zero_s = jnp.zeros((HG_DV, HG_DK), BF16)
    nt_dims = (((1,), (1,)), ((), ()))
    tn_dims = (((0,), (0,)), ((), ()))

    def level_operand(l, qf, kk, f, b_cum, ex):
        blk = c >> l
        half = blk // 2
        if blk >= 2 * SUBLANES:
            pieces, expo = [], []
            for s0 in range(0, c, blk):
                ref = jnp.broadcast_to(b_cum[s0 + half - 1:s0 + half, :], (half, w))
                lo = slice(s0, s0 + half)
                hi = slice(s0 + half, s0 + blk)
                pieces += [kk[lo, :], qf[hi, :]]
                expo += [ref - b_cum[lo, :], b_cum[hi, :] - ref]
            y = jnp.concatenate(pieces, axis=0) * jnp.exp2(jnp.concatenate(expo, axis=0))
        elif l in HG_MXU_LEVELS:
            i = HG_MXU_LEVELS.index(l)
            y = jnp.where(upper_row[l], qf, kk) * jnp.exp2(ex[(1 + i) * c:(2 + i) * c])
        else:
            y = jnp.where(upper_row[l], qf * f, kk)
        return y.astype(BF16)

    st = [st_ref[h] for h in range(HG_PAIR)]
    n_pairs = n_chunks // 2
    fronts, raws, updates, mids, st_ins = {}, {}, {}, {}, {}

    def stage_gates(p):
        out = []
        for ci in (2 * p, 2 * p + 1):
            rs = slice(ci * c, (ci + 1) * c)
            qf = _silu(q_ref[rs, :])
            f = f_mid + f_half * jnp.tanh(0.5 * f_ref[rs, :])
            log_f = jnp.log2(f)
            p0 = log_f.astype(BF16)
            p1 = (log_f - p0.astype(F32)).astype(BF16)
            ex = jnp.dot(m2, jnp.concatenate([p0, p1], axis=0), preferred_element_type=F32)
            out.append((rs, qf, 1.0 - f, f, ex[:c], ex))
        fronts[p] = out

    def stage_levels(p):
        raws[p] = []
        for l in range(N_LEVELS):
            ys = [level_operand(l, qf, kk, f, b_cum, ex) for (_, qf, kk, f, b_cum, ex) in fronts[p]]
            raws[p].append(lax.dot_general(jnp.concatenate(ys, axis=0),
                                           jnp.concatenate([blockdiag(y, zero_c) for y in ys], axis=0),
                                           nt_dims, preferred_element_type=F32))
        updates[p] = []
        for rs, qf, kk, f, b_cum, ex in fronts[p]:
            ke = (kk * jnp.exp2(b_cum[c - 1:c, :] - b_cum)).astype(BF16)
            updates[p].append(lax.dot_general(v_ref[rs, :].astype(BF16), ke, tn_dims,
                                              preferred_element_type=F32))

    def stage_intra(p):
        mids[p] = []
        for j, (rs, qf, kk, f, b_cum, ex) in enumerate(fronts[p]):
            scores = 0.0
            for l in range(N_LEVELS):
                s_l = raws[p][l][j * c:(j + 1) * c, j * HG_PAIR * c:(j + 1) * HG_PAIR * c]
                scores = jnp.where(valid[l], s_l, scores)
            v = v_ref[rs, :]
            o = jnp.dot(scores.astype(BF16), blockdiag(v.astype(BF16), zero_c),
                        preferred_element_type=F32)
            qk = qf * kk
            o = o + per_head(lambda hs: jnp.broadcast_to(
                jnp.sum(qk[:, hs], axis=-1, keepdims=True), (c, HG_DK))) * v
            qe = (qf * jnp.exp2(b_cum)).astype(BF16)
            mids[p].append((rs, o, qe, jnp.exp2(b_cum[c - 1:c, :])))

    def stage_state(p):
        st_ins[p] = []
        for (rs, o, qe, dec), ut in zip(mids[p], updates[p]):
            s0, s1 = (jnp.transpose(s).astype(BF16) for s in st)
            st_ins[p].append(jnp.concatenate([jnp.concatenate([s0, zero_s], axis=1),
                                              jnp.concatenate([zero_s, s1], axis=1)], axis=0))
            for h, hs in enumerate(heads):
                st[h] = st[h] * dec[:, hs] + ut[hs, hs]

    def stage_out(p):
        for (rs, o, qe, dec), st_bd in zip(mids[p], st_ins[p]):
            o = o + jnp.dot(qe, st_bd, preferred_element_type=F32)
            o2 = o * o
            inv = per_head(lambda hs: jnp.broadcast_to(
                lax.rsqrt(jnp.mean(o2[:, hs], axis=-1, keepdims=True) + EPS), (c, HG_DV)))
            o_ref[rs, :] = (o * inv * ng * _silu(g_ref[rs, :])).astype(o_ref.dtype)

    stages = (stage_gates, stage_levels, stage_intra, stage_state, stage_out)
    for tick in range(n_pairs + len(stages) - 1):
        for k, stage in enumerate(stages):
            if 0 <= tick - k < n_pairs:
                stage(tick - k)
    for h in range(HG_PAIR):
        st_ref[h] = st[h]


def _hgrn2(proj, lb_logits, norm_g, layer, col0, width, tb=1024):
    s = proj.shape[0]
    n_heads = width // HG_DK
    bw = HG_PAIR * HG_DK
    assert n_heads % HG_PAIR == 0 and s % tb == 0 and tb % CHUNK == 0 and col0 % bw == 0
    sec = lambda k: (lambda hg, t: (t, (col0 + k * width) // bw + hg))
    depth1 = lb_logits.shape[0]
    m2 = jnp.asarray(_decay_sum_matrix(), BF16)
    blk = 4 * tb * bw * 4 + tb * bw * 2 + m2.size * 2
    return pl.pallas_call(
        functools.partial(_hgrn_kernel, tb=tb, layer=layer),
        grid=(n_heads // HG_PAIR, s // tb),
        in_specs=[pl.BlockSpec((tb, bw), sec(0)),
                  pl.BlockSpec((tb, bw), sec(1)),
                  pl.BlockSpec((tb, bw), sec(2)),
                  pl.BlockSpec((tb, bw), sec(3)),
                  pl.BlockSpec((depth1, bw), lambda hg, t: (0, hg)),
                  pl.BlockSpec((1, bw), lambda hg, t: (0, hg)),
                  pl.BlockSpec(m2.shape, lambda hg, t: (0, 0))],
        out_specs=pl.BlockSpec((tb, bw), lambda hg, t: (t, hg)),
        out_shape=jax.ShapeDtypeStruct((s, width), BF16),
        scratch_shapes=[pltpu.VMEM((HG_PAIR, HG_DV, HG_DK), F32)],
        compiler_params=pltpu.CompilerParams(
            dimension_semantics=("arbitrary", "arbitrary"),
            vmem_limit_bytes=_vmem_limit(blk + 8 * 1024 * 1024)),
        name="hgrn2",
    )(proj, proj, proj, proj, lb_logits, norm_g.reshape(1, width), m2)


def kernel(x, attn_norm_g, w_in, conv_w, conv_b, conv_ln_g, conv_ln_b, hg_lb_logits, hg_norm_g,
           w_out, ffn_norm_g, w_gate, w_up, w_down, final_norm_g):
    bsz, seq, d_model = x.shape
    assert bsz == 1
    depth = w_in.shape[0]
    conv_width = conv_w.shape[2]
    hg_width = hg_norm_g.shape[1]
    d_ff = w_gate.shape[2]
    bf = 256
    assert d_ff % bf == 0

    xs = x.reshape(seq, d_model)
    for l in range(depth):
        h = _rmsnorm(xs, attn_norm_g[l], BF16)
        proj, w_out_b, w_down_b = _matmul([h], w_in[l], None, F32, bm=2048, bn=512, lhs_bufs=1,
                                          side_casts=(w_out[l], w_down[l]), side_rows=128)
        a_out = _conformer_conv(proj, conv_w[l], conv_b[l], conv_ln_g[l], conv_ln_b[l], conv_width)
        b_out = _hgrn2(proj, hg_lb_logits, hg_norm_g[l], l, 2 * conv_width, hg_width)
        xs = _matmul([a_out, b_out], w_out_b, xs, F32, bm=1024, bn=1024)
        h = _rmsnorm(xs, ffn_norm_g[l], BF16)
        u = _swiglu(h, w_gate[l], w_up[l], bm=2048, bf=bf)
        xs = _matmul([u], w_down_b, xs, F32, bm=512, bn=512)
    out = _rmsnorm(xs, final_norm_g, F32)
    return out.reshape(bsz, seq, d_model)
```

```python
import functools

import numpy as np
import jax
import jax.numpy as jnp
from jax import lax
from jax.experimental import pallas as pl
from jax.experimental.pallas import tpu as pltpu

F32 = jnp.float32
BF16 = jnp.bfloat16
EPS = 1e-6

LANES = 128
SUBLANES = 8
VMEM_BYTES = 64 * 1024 * 1024
VMEM_HEADROOM = 6 * 1024 * 1024

HG_DK = 128
HG_DV = 128
HG_PAIR = 2
CHUNK = 64
N_LEVELS = 6
CONV_K = 31
CONV_HALO = 32


def _vmem_limit(block_bytes):
    return int(min(VMEM_BYTES - 2 * 1024 * 1024, 2 * block_bytes + VMEM_HEADROOM))


def _sigmoid(x):
    return 1.0 / (1.0 + jnp.exp(-x))


def _silu(x):
    hx = 0.5 * x
    return hx + hx * jnp.tanh(hx)


def _rmsnorm_kernel(x_ref, g_ref, o_ref):
    x = x_ref[...]
    ms = jnp.mean(x * x, axis=-1, keepdims=True)
    o_ref[...] = (x * lax.rsqrt(ms + EPS) * g_ref[...]).astype(o_ref.dtype)


def _rmsnorm(x, g, out_dtype, bm=512):
    m, d = x.shape
    return pl.pallas_call(
        _rmsnorm_kernel,
        grid=(m // bm,),
        in_specs=[pl.BlockSpec((bm, d), lambda i: (i, 0)),
                  pl.BlockSpec((1, d), lambda i: (0, 0))],
        out_specs=pl.BlockSpec((bm, d), lambda i: (i, 0)),
        out_shape=jax.ShapeDtypeStruct((m, d), out_dtype),
        compiler_params=pltpu.CompilerParams(
            dimension_semantics=("arbitrary",),
            vmem_limit_bytes=_vmem_limit(bm * d * 10)),
        name="rmsnorm",
    )(x, g.reshape(1, d))


def _matmul_kernel(*refs, n_lhs, has_res, nk, n_side):
    lhs = refs[:n_lhs]
    ws = refs[n_lhs:2 * n_lhs]
    res_ref = refs[2 * n_lhs] if has_res else None
    n_in = 2 * n_lhs + has_res
    o_ref = refs[n_in + n_side]
    for s_in, s_out in zip(refs[n_in:n_in + n_side], refs[n_in + n_side + 1:]):
        s_out[...] = s_in[...].astype(s_out.dtype)
    acc = None
    for a_ref, w_ref in zip(lhs, ws):
        d = jnp.dot(a_ref[...], w_ref[...].astype(BF16), preferred_element_type=F32)
        acc = d if acc is None else acc + d
    if nk == 1:
        if has_res:
            acc = acc + res_ref[...]
        if len(o_ref.shape) == 3:
            ct = o_ref.shape[2]
            for t in range(o_ref.shape[0]):
                o_ref[t] = acc[:, t * ct:(t + 1) * ct].astype(o_ref.dtype)
        else:
            o_ref[...] = acc.astype(o_ref.dtype)
    else:
        k = pl.program_id(2)

        @pl.when(k == 0)
        def _():
            o_ref[...] = acc + res_ref[...] if has_res else acc

        @pl.when(k > 0)
        def _():
            o_ref[...] += acc


def _matmul(lhs_list, w, res, out_dtype, bm, bn, bk=None, lhs_bufs=2, side_casts=(), side_rows=256,
            out_col_tile=None):
    m, kdim = lhs_list[0].shape
    n = w.shape[1]
    bk = kdim if bk is None else bk
    nk = kdim // bk
    n_lhs = len(lhs_list)
    assert m % bm == 0 and n % bn == 0 and kdim % bk == 0 and w.shape[0] == n_lhs * kdim
    assert nk == 1 or out_dtype == F32
    w_bytes = jnp.dtype(w.dtype).itemsize
    w_spec = lambda p: pl.BlockSpec((bk, bn), lambda i, j, k: (p * nk + k, j))
    assert lhs_bufs == 2 or nk == 1
    in_specs = ([pl.BlockSpec((bm, bk), lambda i, j, k: (i, k), pipeline_mode=pl.Buffered(lhs_bufs))] * n_lhs
                + [w_spec(p) for p in range(n_lhs)])
    args = list(lhs_list) + [w] * n_lhs
    w_tmp = 1 if w.dtype != BF16 else 0
    blk = (n_lhs * (bm * bk * lhs_bufs + bk * bn * (w_bytes + w_tmp))
           + bm * bn * jnp.dtype(out_dtype).itemsize)
    if res is not None:
        in_specs.append(pl.BlockSpec((bm, bn), lambda i, j, k: (i, j)))
        args.append(res)
        blk += bm * bn * 4
    nj = n // bn
    if out_col_tile is None:
        out_specs = [pl.BlockSpec((bm, bn), lambda i, j, k: (i, j))]
        out_shape = [jax.ShapeDtypeStruct((m, n), out_dtype)]
    else:
        assert nk == 1 and bn % out_col_tile == 0
        out_specs = [pl.BlockSpec((bn // out_col_tile, bm, out_col_tile), lambda i, j, k: (j, i, 0))]
        out_shape = [jax.ShapeDtypeStruct((n // out_col_tile, m, out_col_tile), out_dtype)]
    for a in side_casts:
        rows, cols = a.shape
        n_blk = rows // side_rows
        assert nk == 1 and rows % side_rows == 0 and n_blk <= (m // bm) * nj
        side_map = lambda i, j, k, n_blk=n_blk: (jnp.minimum(i * nj + j, n_blk - 1), 0)
        in_specs.append(pl.BlockSpec((side_rows, cols), side_map))
        out_specs.append(pl.BlockSpec((side_rows, cols), side_map))
        out_shape.append(jax.ShapeDtypeStruct((rows, cols), BF16))
        args.append(a)
        blk += side_rows * cols * 6
    outs = pl.pallas_call(
        functools.partial(_matmul_kernel, n_lhs=n_lhs, has_res=res is not None, nk=nk,
                          n_side=len(side_casts)),
        grid=(m // bm, nj, nk),
        in_specs=in_specs,
        out_specs=out_specs,
        out_shape=out_shape,
        compiler_params=pltpu.CompilerParams(
            dimension_semantics=("arbitrary", "arbitrary", "arbitrary"),
            vmem_limit_bytes=_vmem_limit(blk + bm * bn * 4)),
        name="matmul",
    )(*args)
    return outs if side_casts else outs[0]


SWIGLU_ROW_SPLITS = 4


def _swiglu_kernel(h_ref, wg_ref, wu_ref, u_ref):
    wg = wg_ref[...].astype(BF16)
    wu = wu_ref[...].astype(BF16)
    rows = h_ref.shape[0] // SWIGLU_ROW_SPLITS
    for r in range(SWIGLU_ROW_SPLITS):
        rs = slice(r * rows, (r + 1) * rows)
        h = h_ref[rs, :]
        g = jnp.dot(h, wg, preferred_element_type=F32)
        u = jnp.dot(h, wu, preferred_element_type=F32)
        u_ref[rs, :] = (_silu(g) * u).astype(u_ref.dtype)


def _swiglu(h, w_gate, w_up, bm, bf):
    m, d = h.shape
    d_ff = w_gate.shape[1]
    assert m % bm == 0 and d_ff % bf == 0
    w_bytes = jnp.dtype(w_gate.dtype).itemsize
    blk = bm * d + 2 * d * bf * (w_bytes + 1) + bm * bf * 2
    return pl.pallas_call(
        _swiglu_kernel,
        grid=(m // bm, d_ff // bf),
        in_specs=[pl.BlockSpec((bm, d), lambda i, j: (i, 0), pipeline_mode=pl.Buffered(1)),
                  pl.BlockSpec((d, bf), lambda i, j: (0, j)),
                  pl.BlockSpec((d, bf), lambda i, j: (0, j))],
        out_specs=pl.BlockSpec((bm, bf), lambda i, j: (i, j)),
        out_shape=jax.ShapeDtypeStruct((m, d_ff), BF16),
        compiler_params=pltpu.CompilerParams(
            dimension_semantics=("arbitrary", "arbitrary"),
            vmem_limit_bytes=_vmem_limit(blk + bm * 2 * bf * 4)),
        name="swiglu",
    )(h, w_gate, w_up)


CONV_CW = 128
CONV_R = 128


def _conv_kernel(xv_ref, xg_ref, hv_ref, hg_ref, w_ref, b_ref, lg_ref, lb_ref, o_ref,
                 hext_ref, sh_ref, y_ref, *, t_rows, width):
    i = pl.program_id(0)
    ct = xv_ref.shape[2]
    for t in range(xv_ref.shape[0]):
        ts = slice(t * ct, (t + 1) * ct)
        hext_ref[CONV_HALO:, ts] = xv_ref[t] * _sigmoid(xg_ref[t])
        halo = hv_ref[t] * _sigmoid(hg_ref[t])
        hext_ref[:CONV_HALO, ts] = jnp.where(i > 0, halo, 0.0)

    first = CONV_HALO - (CONV_K - 1)
    n_sh = t_rows + CONV_HALO - SUBLANES
    for c in range(width // CONV_CW):
        cs = slice(c * CONV_CW, (c + 1) * CONV_CW)
        for b in range(1, SUBLANES):
            sh_ref[b - 1] = hext_ref[pl.ds(b, n_sh), cs]

        def row_tile(r, carry, cs=cs):
            r0 = pl.multiple_of(r * CONV_R, CONV_R)
            acc = jnp.zeros((CONV_R, CONV_CW), F32)
            for k in range(CONV_K):
                a, b = divmod(first + k, SUBLANES)
                if b == 0:
                    src = hext_ref[pl.ds(r0 + SUBLANES * a, CONV_R), cs]
                else:
                    src = sh_ref[b - 1, pl.ds(r0 + SUBLANES * a, CONV_R), :]
                acc = acc + w_ref[k:k + 1, cs] * src
            y_ref[pl.ds(r0, CONV_R), cs] = acc + b_ref[:, cs]
            return carry

        lax.fori_loop(0, t_rows // CONV_R, row_tile, 0)

    y = y_ref[...]
    mu = jnp.mean(y, axis=-1, keepdims=True)
    yc = y - mu
    var = jnp.mean(yc * yc, axis=-1, keepdims=True)
    z = yc * lax.rsqrt(var + EPS) * lg_ref[...] + lb_ref[...]
    o_ref[...] = (z * _sigmoid(z)).astype(o_ref.dtype)


def _conformer_conv(proj, conv_w, conv_b, ln_g, ln_b, width, t_rows=256):
    _, s, ct = proj.shape
    nt = width // ct
    hb = t_rows // CONV_HALO
    row = lambda a: a.reshape(1, width)
    halo_map = lambda c: (lambda i: (c, jnp.maximum(i * hb - 1, 0), 0))
    blk = (2 * t_rows + 2 * CONV_HALO + 40) * width * 4 + t_rows * width * 2
    scratch = ((t_rows + CONV_HALO) * width + 7 * (t_rows + CONV_HALO) * CONV_CW + t_rows * width) * 4
    return pl.pallas_call(
        functools.partial(_conv_kernel, t_rows=t_rows, width=width),
        grid=(s // t_rows,),
        in_specs=[pl.BlockSpec((nt, t_rows, ct), lambda i: (0, i, 0)),
                  pl.BlockSpec((nt, t_rows, ct), lambda i: (1, i, 0)),
                  pl.BlockSpec((nt, CONV_HALO, ct), halo_map(0)),
                  pl.BlockSpec((nt, CONV_HALO, ct), halo_map(1)),
                  pl.BlockSpec((CONV_K, width), lambda i: (0, 0)),
                  pl.BlockSpec((1, width), lambda i: (0, 0)),
                  pl.BlockSpec((1, width), lambda i: (0, 0)),
                  pl.BlockSpec((1, width), lambda i: (0, 0))],
        out_specs=pl.BlockSpec((t_rows, width), lambda i: (i, 0)),
        out_shape=jax.ShapeDtypeStruct((s, width), BF16),
        scratch_shapes=[pltpu.VMEM((t_rows + CONV_HALO, width), F32),
                        pltpu.VMEM((SUBLANES - 1, t_rows + CONV_HALO - SUBLANES, CONV_CW), F32),
                        pltpu.VMEM((t_rows, width), F32)],
        compiler_params=pltpu.CompilerParams(
            dimension_semantics=("arbitrary",),
            vmem_limit_bytes=_vmem_limit(blk + scratch)),
        name="conformer_conv",
    )(proj, proj, proj, proj, conv_w, row(conv_b), row(ln_g), row(ln_b))


HG_MXU_LEVELS = (3, 4)


def _decay_sum_matrix():
    c = CHUNK
    m = np.zeros((1 + len(HG_MXU_LEVELS), c, c), np.float32)
    for t in range(c):
        m[0, t, :t + 1] = 1.0
        for i, l in enumerate(HG_MXU_LEVELS):
            blk = c >> l
            mid = t - t % blk + blk // 2
            if t >= mid:
                m[1 + i, t, mid:t + 1] = 1.0
            else:
                m[1 + i, t, t + 1:mid] = 1.0
    m = m.reshape(-1, c)
    return np.concatenate([m, m], axis=1)


def _hgrn_kernel(q_ref, f_ref, v_ref, g_ref, lbl_ref, ng_ref, m_ref, o_ref, st_ref, *, tb, layer):
    t_blk = pl.program_id(1)

    @pl.when(t_blk == 0)
    def _():
        st_ref[...] = jnp.zeros_like(st_ref)

    c = CHUNK
    n_chunks = tb // c
    w = HG_PAIR * HG_DK
    heads = [slice(h * HG_DK, (h + 1) * HG_DK) for h in range(HG_PAIR)]
    lbl = lbl_ref[...]
    e = jnp.exp(lbl - jnp.max(lbl, axis=0, keepdims=True))
    lb = jnp.sum(e[:layer + 1], axis=0, keepdims=True) / jnp.sum(e, axis=0, keepdims=True)
    f_half = 0.5 * (1.0 - lb)
    f_mid = lb + f_half
    ng = ng_ref[...]
    m2 = m_ref[...]

    row = lax.broadcasted_iota(jnp.int32, (c, w), 0)
    upper_row = {l: (row % (c >> l)) >= (c >> (l + 1)) for l in range(3, N_LEVELS)}
    t_i = lax.broadcasted_iota(jnp.int32, (c, HG_PAIR * c), 0)
    s_i = lax.broadcasted_iota(jnp.int32, (c, HG_PAIR * c), 1) % c
    valid = []
    for l in range(N_LEVELS):
        blk = c >> l
        valid.append((t_i // blk == s_i // blk) & (t_i % blk >= blk // 2) & (s_i % blk < blk // 2))

    def blockdiag(a, zero):
        return jnp.concatenate([jnp.concatenate([a[:, heads[0]], zero], axis=1),
                                jnp.concatenate([zero, a[:, heads[1]]], axis=1)], axis=0)

    def per_head(fn):
        return jnp.concatenate([fn(hs) for hs in heads], axis=1)

    zero_c = jnp.zeros((c, HG_DK), BF16)
    zero_s = jnp.zeros((HG_DV, HG_DK), BF16)
    nt_dims = (((1,), (1,)), ((), ()))
    tn_dims = (((0,), (0,)), ((), ()))

    def level_operand(l, qf, kk, f, b_cum, ex):
        blk = c >> l
        half = blk // 2
        if blk >= 2 * SUBLANES:
            pieces, expo = [], []
            for s0 in range(0, c, blk):
                ref = jnp.broadcast_to(b_cum[s0 + half - 1:s0 + half, :], (half, w))
                lo = slice(s0, s0 + half)
                hi = slice(s0 + half, s0 + blk)
                pieces += [kk[lo, :], qf[hi, :]]
                expo += [ref - b_cum[lo, :], b_cum[hi, :] - ref]
            y = jnp.concatenate(pieces, axis=0) * jnp.exp2(jnp.concatenate(expo, axis=0))
        elif l in HG_MXU_LEVELS:
            i = HG_MXU_LEVELS.index(l)
            y = jnp.where(upper_row[l], qf, kk) * jnp.exp2(ex[(1 + i) * c:(2 + i) * c])
        else:
            y = jnp.where(upper_row[l], qf * f, kk)
        return y.astype(BF16)

    st = [st_ref[h] for h in range(HG_PAIR)]
    n_pairs = n_chunks // 2
    fronts, raws, updates, mids, st_ins = {}, {}, {}, {}, {}

    def stage_gates(p):
        out = []
        for ci in (2 * p, 2 * p + 1):
            rs = slice(ci * c, (ci + 1) * c)
            qf = _silu(q_ref[rs, :])
            f = f_mid + f_half * jnp.tanh(0.5 * f_ref[rs, :])
            log_f = jnp.log2(f)
            p0 = log_f.astype(BF16)
            p1 = (log_f - p0.astype(F32)).astype(BF16)
            ex = jnp.dot(m2, jnp.concatenate([p0, p1], axis=0), preferred_element_type=F32)
            out.append((rs, qf, 1.0 - f, f, ex[:c], ex))
        fronts[p] = out

    def stage_levels(p):
        raws[p] = []
        for l in range(N_LEVELS):
            ys = [level_operand(l, qf, kk, f, b_cum, ex) for (_, qf, kk, f, b_cum, ex) in fronts[p]]
            raws[p].append(lax.dot_general(jnp.concatenate(ys, axis=0),
                                           jnp.concatenate([blockdiag(y, zero_c) for y in ys], axis=0),
                                           nt_dims, preferred_element_type=F32))
        updates[p] = []
        for rs, qf, kk, f, b_cum, ex in fronts[p]:
            ke = (kk * jnp.exp2(b_cum[c - 1:c, :] - b_cum)).astype(BF16)
            updates[p].append(lax.dot_general(v_ref[rs, :].astype(BF16), ke, tn_dims,
                                              preferred_element_type=F32))

    def stage_intra(p):
        mids[p] = []
        for j, (rs, qf, kk, f, b_cum, ex) in enumerate(fronts[p]):
            scores = 0.0
            for l in range(N_LEVELS):
                s_l = raws[p][l][j * c:(j + 1) * c, j * HG_PAIR * c:(j + 1) * HG_PAIR * c]
                scores = jnp.where(valid[l], s_l, scores)
            v = v_ref[rs, :]
            o = jnp.dot(scores.astype(BF16), blockdiag(v.astype(BF16), zero_c),
                        preferred_element_type=F32)
            qk = qf * kk
            o = o + per_head(lambda hs: jnp.broadcast_to(
                jnp.sum(qk[:, hs], axis=-1, keepdims=True), (c, HG_DK))) * v
            qe = (qf * jnp.exp2(b_cum)).astype(BF16)
            mids[p].append((rs, o, qe, jnp.exp2(b_cum[c - 1:c, :])))

    def stage_state(p):
        st_ins[p] = []
        for (rs, o, qe, dec), ut in zip(mids[p], updates[p]):
            s0, s1 = (jnp.transpose(s).astype(BF16) for s in st)
            st_ins[p].append(jnp.concatenate([jnp.concatenate([s0, zero_s], axis=1),
                                              jnp.concatenate([zero_s, s1], axis=1)], axis=0))
            for h, hs in enumerate(heads):
                st[h] = st[h] * dec[:, hs] + ut[hs, hs]

    def stage_out(p):
        for (rs, o, qe, dec), st_bd in zip(mids[p], st_ins[p]):
            o = o + jnp.dot(qe, st_bd, preferred_element_type=F32)
            o2 = o * o
            inv = per_head(lambda hs: jnp.broadcast_to(
                lax.rsqrt(jnp.mean(o2[:, hs], axis=-1, keepdims=True) + EPS), (c, HG_DV)))
            o_ref[rs, :] = (o * inv * ng * _silu(g_ref[rs, :])).astype(o_ref.dtype)

    stages = (stage_gates, stage_levels, stage_intra, stage_state, stage_out)
    for tick in range(n_pairs + len(stages) - 1):
        for k, stage in enumerate(stages):
            if 0 <= tick - k < n_pairs:
                stage(tick - k)
    for h in range(HG_PAIR):
        st_ref[h] = st[h]


def _hgrn2(proj, lb_logits, norm_g, layer, col0, width, tb=1024):
    _, s, bw = proj.shape
    n_heads = width // HG_DK
    assert bw == HG_PAIR * HG_DK
    assert n_heads % HG_PAIR == 0 and s % tb == 0 and tb % CHUNK == 0 and col0 % bw == 0
    sec = lambda k: (lambda hg, t: ((col0 + k * width) // bw + hg, t, 0))
    depth1 = lb_logits.shape[0]
    m2 = jnp.asarray(_decay_sum_matrix(), BF16)
    blk = 4 * tb * bw * 4 + tb * bw * 2 + m2.size * 2
    return pl.pallas_call(
        functools.partial(_hgrn_kernel, tb=tb, layer=layer),
        grid=(n_heads // HG_PAIR, s // tb),
        in_specs=[pl.BlockSpec((None, tb, bw), sec(0)),
                  pl.BlockSpec((None, tb, bw), sec(1)),
                  pl.BlockSpec((None, tb, bw), sec(2)),
                  pl.BlockSpec((None, tb, bw), sec(3)),
                  pl.BlockSpec((depth1, bw), lambda hg, t: (0, hg)),
                  pl.BlockSpec((1, bw), lambda hg, t: (0, hg)),
                  pl.BlockSpec(m2.shape, lambda hg, t: (0, 0))],
        out_specs=pl.BlockSpec((tb, bw), lambda hg, t: (t, hg)),
        out_shape=jax.ShapeDtypeStruct((s, width), BF16),
        scratch_shapes=[pltpu.VMEM((HG_PAIR, HG_DV, HG_DK), F32)],
        compiler_params=pltpu.CompilerParams(
            dimension_semantics=("arbitrary", "arbitrary"),
            vmem_limit_bytes=_vmem_limit(blk + 8 * 1024 * 1024)),
        name="hgrn2",
    )(proj, proj, proj, proj, lb_logits, norm_g.reshape(1, width), m2)


def kernel(x, attn_norm_g, w_in, conv_w, conv_b, conv_ln_g, conv_ln_b, hg_lb_logits, hg_norm_g,
           w_out, ffn_norm_g, w_gate, w_up, w_down, final_norm_g):
    bsz, seq, d_model = x.shape
    assert bsz == 1
    depth = w_in.shape[0]
    conv_width = conv_w.shape[2]
    hg_width = hg_norm_g.shape[1]
    d_ff = w_gate.shape[2]
    bf = 256
    assert d_ff % bf == 0

    xs = x.reshape(seq, d_model)
    for l in range(depth):
        h = _rmsnorm(xs, attn_norm_g[l], BF16)
        proj, w_out_b, w_down_b = _matmul([h], w_in[l], None, F32, bm=2048, bn=512, lhs_bufs=1,
                                          side_casts=(w_out[l], w_down[l]), side_rows=128,
                                          out_col_tile=HG_PAIR * HG_DK)
        a_out = _conformer_conv(proj, conv_w[l], conv_b[l], conv_ln_g[l], conv_ln_b[l], conv_width)
        b_out = _hgrn2(proj, hg_lb_logits, hg_norm_g[l], l, 2 * conv_width, hg_width)
        xs = _matmul([a_out, b_out], w_out_b, xs, F32, bm=1024, bn=1024)
        h = _rmsnorm(xs, ffn_norm_g[l], BF16)
        u = _swiglu(h, w_gate[l], w_up[l], bm=2048, bf=bf)
        xs = _matmul([u], w_down_b, xs, F32, bm=512, bn=512)
    out = _rmsnorm(xs, final_norm_g, F32)
    return out.reshape(bsz, seq, d_model)
```

```python
import functools

import numpy as np
import jax
import jax.numpy as jnp
from jax import lax
from jax.experimental import pallas as pl
from jax.experimental.pallas import tpu as pltpu

F32 = jnp.float32
BF16 = jnp.bfloat16
EPS = 1e-6

LANES = 128
SUBLANES = 8
VMEM_BYTES = 64 * 1024 * 1024
VMEM_HEADROOM = 6 * 1024 * 1024

HG_DK = 128
HG_DV = 128
HG_PAIR = 2
CHUNK = 64
N_LEVELS = 6
CONV_K = 31
CONV_HALO = 32


def _vmem_limit(block_bytes):
    return int(min(VMEM_BYTES - 2 * 1024 * 1024, 2 * block_bytes + VMEM_HEADROOM))


def _sigmoid(x):
    return 1.0 / (1.0 + jnp.exp(-x))


def _silu(x):
    hx = 0.5 * x
    return hx + hx * jnp.tanh(hx)


def _rmsnorm_kernel(x_ref, g_ref, o_ref):
    x = x_ref[...]
    ms = jnp.mean(x * x, axis=-1, keepdims=True)
    o_ref[...] = (x * lax.rsqrt(ms + EPS) * g_ref[...]).astype(o_ref.dtype)


def _rmsnorm(x, g, out_dtype, bm=512):
    m, d = x.shape
    return pl.pallas_call(
        _rmsnorm_kernel,
        grid=(m // bm,),
        in_specs=[pl.BlockSpec((bm, d), lambda i: (i, 0)),
                  pl.BlockSpec((1, d), lambda i: (0, 0))],
        out_specs=pl.BlockSpec((bm, d), lambda i: (i, 0)),
        out_shape=jax.ShapeDtypeStruct((m, d), out_dtype),
        compiler_params=pltpu.CompilerParams(
            dimension_semantics=("arbitrary",),
            vmem_limit_bytes=_vmem_limit(bm * d * 10)),
        name="rmsnorm",
    )(x, g.reshape(1, d))


def _matmul_kernel(*refs, n_lhs, has_res, nk, n_side):
    lhs = refs[:n_lhs]
    ws = refs[n_lhs:2 * n_lhs]
    res_ref = refs[2 * n_lhs] if has_res else None
    n_in = 2 * n_lhs + has_res
    o_ref = refs[n_in + n_side]
    for s_in, s_out in zip(refs[n_in:n_in + n_side], refs[n_in + n_side + 1:]):
        s_out[...] = s_in[...].astype(s_out.dtype)
    acc = None
    for a_ref, w_ref in zip(lhs, ws):
        d = jnp.dot(a_ref[...], w_ref[...].astype(BF16), preferred_element_type=F32)
        acc = d if acc is None else acc + d
    if nk == 1:
        if has_res:
            acc = acc + res_ref[...]
        if len(o_ref.shape) == 3:
            ct = o_ref.shape[2]
            for t in range(o_ref.shape[0]):
                o_ref[t] = acc[:, t * ct:(t + 1) * ct].astype(o_ref.dtype)
        else:
            o_ref[...] = acc.astype(o_ref.dtype)
    else:
        k = pl.program_id(2)

        @pl.when(k == 0)
        def _():
            o_ref[...] = acc + res_ref[...] if has_res else acc

        @pl.when(k > 0)
        def _():
            o_ref[...] += acc


def _matmul(lhs_list, w, res, out_dtype, bm, bn, bk=None, lhs_bufs=2, side_casts=(), side_rows=256,
            out_col_tile=None):
    m, kdim = lhs_list[0].shape
    n = w.shape[1]
    bk = kdim if bk is None else bk
    nk = kdim // bk
    n_lhs = len(lhs_list)
    assert m % bm == 0 and n % bn == 0 and kdim % bk == 0 and w.shape[0] == n_lhs * kdim
    assert nk == 1 or out_dtype == F32
    w_bytes = jnp.dtype(w.dtype).itemsize
    w_spec = lambda p: pl.BlockSpec((bk, bn), lambda i, j, k: (p * nk + k, j))
    assert lhs_bufs == 2 or nk == 1
    in_specs = ([pl.BlockSpec((bm, bk), lambda i, j, k: (i, k), pipeline_mode=pl.Buffered(lhs_bufs))] * n_lhs
                + [w_spec(p) for p in range(n_lhs)])
    args = list(lhs_list) + [w] * n_lhs
    w_tmp = 1 if w.dtype != BF16 else 0
    blk = (n_lhs * (bm * bk * lhs_bufs + bk * bn * (w_bytes + w_tmp))
           + bm * bn * jnp.dtype(out_dtype).itemsize)
    if res is not None:
        in_specs.append(pl.BlockSpec((bm, bn), lambda i, j, k: (i, j)))
        args.append(res)
        blk += bm * bn * 4
    nj = n // bn
    if out_col_tile is None:
        out_specs = [pl.BlockSpec((bm, bn), lambda i, j, k: (i, j))]
        out_shape = [jax.ShapeDtypeStruct((m, n), out_dtype)]
    else:
        assert nk == 1 and bn % out_col_tile == 0
        out_specs = [pl.BlockSpec((bn // out_col_tile, bm, out_col_tile), lambda i, j, k: (j, i, 0))]
        out_shape = [jax.ShapeDtypeStruct((n // out_col_tile, m, out_col_tile), out_dtype)]
    for a in side_casts:
        rows, cols = a.shape
        n_blk = rows // side_rows
        assert nk == 1 and rows % side_rows == 0 and n_blk <= (m // bm) * nj
        side_map = lambda i, j, k, n_blk=n_blk: (jnp.minimum(i * nj + j, n_blk - 1), 0)
        in_specs.append(pl.BlockSpec((side_rows, cols), side_map))
        out_specs.append(pl.BlockSpec((side_rows, cols), side_map))
        out_shape.append(jax.ShapeDtypeStruct((rows, cols), BF16))
        args.append(a)
        blk += side_rows * cols * 6
    outs = pl.pallas_call(
        functools.partial(_matmul_kernel, n_lhs=n_lhs, has_res=res is not None, nk=nk,
                          n_side=len(side_casts)),
        grid=(m // bm, nj, nk),
        in_specs=in_specs,
        out_specs=out_specs,
        out_shape=out_shape,
        compiler_params=pltpu.CompilerParams(
            dimension_semantics=("arbitrary", "arbitrary", "arbitrary"),
            vmem_limit_bytes=_vmem_limit(blk + bm * bn * 4)),
        name="matmul",
    )(*args)
    return outs if side_casts else outs[0]


SWIGLU_ROW_SPLITS = 4


def _swiglu_kernel(h_ref, wg_ref, wu_ref, u_ref):
    wg = wg_ref[...].astype(BF16)
    wu = wu_ref[...].astype(BF16)
    rows = h_ref.shape[0] // SWIGLU_ROW_SPLITS
    for r in range(SWIGLU_ROW_SPLITS):
        rs = slice(r * rows, (r + 1) * rows)
        h = h_ref[rs, :]
        g = jnp.dot(h, wg, preferred_element_type=F32)
        u = jnp.dot(h, wu, preferred_element_type=F32)
        u_ref[rs, :] = (_silu(g) * u).astype(u_ref.dtype)


def _swiglu(h, w_gate, w_up, bm, bf):
    m, d = h.shape
    d_ff = w_gate.shape[1]
    assert m % bm == 0 and d_ff % bf == 0
    w_bytes = jnp.dtype(w_gate.dtype).itemsize
    blk = bm * d + 2 * d * bf * (w_bytes + 1) + bm * bf * 2
    return pl.pallas_call(
        _swiglu_kernel,
        grid=(m // bm, d_ff // bf),
        in_specs=[pl.BlockSpec((bm, d), lambda i, j: (i, 0), pipeline_mode=pl.Buffered(1)),
                  pl.BlockSpec((d, bf), lambda i, j: (0, j)),
                  pl.BlockSpec((d, bf), lambda i, j: (0, j))],
        out_specs=pl.BlockSpec((bm, bf), lambda i, j: (i, j)),
        out_shape=jax.ShapeDtypeStruct((m, d_ff), BF16),
        compiler_params=pltpu.CompilerParams(
            dimension_semantics=("arbitrary", "arbitrary"),
            vmem_limit_bytes=_vmem_limit(blk + bm * 2 * bf * 4)),
        name="swiglu",
    )(h, w_gate, w_up)


CONV_CW = 128
CONV_R = 128


def _conv_kernel(xv_ref, xg_ref, hv_ref, hg_ref, w_ref, b_ref, lg_ref, lb_ref, o_ref,
                 hext_ref, sh_ref, y_ref, *, t_rows, width):
    i = pl.program_id(0)
    ct = xv_ref.shape[2]
    for t in range(xv_ref.shape[0]):
        ts = slice(t * ct, (t + 1) * ct)
        hext_ref[CONV_HALO:, ts] = xv_ref[t] * _sigmoid(xg_ref[t])
        halo = hv_ref[t] * _sigmoid(hg_ref[t])
        hext_ref[:CONV_HALO, ts] = jnp.where(i > 0, halo, 0.0)

    first = CONV_HALO - (CONV_K - 1)
    n_sh = t_rows + CONV_HALO - SUBLANES
    for c in range(width // CONV_CW):
        cs = slice(c * CONV_CW, (c + 1) * CONV_CW)
        for b in range(1, SUBLANES):
            sh_ref[b - 1] = hext_ref[pl.ds(b, n_sh), cs]

        def row_tile(r, carry, cs=cs):
            r0 = pl.multiple_of(r * CONV_R, CONV_R)
            acc = jnp.zeros((CONV_R, CONV_CW), F32)
            for k in range(CONV_K):
                a, b = divmod(first + k, SUBLANES)
                if b == 0:
                    src = hext_ref[pl.ds(r0 + SUBLANES * a, CONV_R), cs]
                else:
                    src = sh_ref[b - 1, pl.ds(r0 + SUBLANES * a, CONV_R), :]
                acc = acc + w_ref[k:k + 1, cs] * src
            y_ref[pl.ds(r0, CONV_R), cs] = acc + b_ref[:, cs]
            return carry

        lax.fori_loop(0, t_rows // CONV_R, row_tile, 0)

    y = y_ref[...]
    mu = jnp.mean(y, axis=-1, keepdims=True)
    yc = y - mu
    var = jnp.mean(yc * yc, axis=-1, keepdims=True)
    z = yc * lax.rsqrt(var + EPS) * lg_ref[...] + lb_ref[...]
    o_ref[...] = (z * _sigmoid(z)).astype(o_ref.dtype)


def _conformer_conv(proj, conv_w, conv_b, ln_g, ln_b, width, t_rows=256):
    _, s, ct = proj.shape
    nt = width // ct
    hb = t_rows // CONV_HALO
    row = lambda a: a.reshape(1, width)
    halo_map = lambda c: (lambda i: (c, jnp.maximum(i * hb - 1, 0), 0))
    blk = (2 * t_rows + 2 * CONV_HALO + 40) * width * 4 + t_rows * width * 2
    scratch = ((t_rows + CONV_HALO) * width + 7 * (t_rows + CONV_HALO) * CONV_CW + t_rows * width) * 4
    return pl.pallas_call(
        functools.partial(_conv_kernel, t_rows=t_rows, width=width),
        grid=(s // t_rows,),
        in_specs=[pl.BlockSpec((nt, t_rows, ct), lambda i: (0, i, 0)),
                  pl.BlockSpec((nt, t_rows, ct), lambda i: (1, i, 0)),
                  pl.BlockSpec((nt, CONV_HALO, ct), halo_map(0)),
                  pl.BlockSpec((nt, CONV_HALO, ct), halo_map(1)),
                  pl.BlockSpec((CONV_K, width), lambda i: (0, 0)),
                  pl.BlockSpec((1, width), lambda i: (0, 0)),
                  pl.BlockSpec((1, width), lambda i: (0, 0)),
                  pl.BlockSpec((1, width), lambda i: (0, 0))],
        out_specs=pl.BlockSpec((t_rows, width), lambda i: (i, 0)),
        out_shape=jax.ShapeDtypeStruct((s, width), BF16),
        scratch_shapes=[pltpu.VMEM((t_rows + CONV_HALO, width), F32),
                        pltpu.VMEM((SUBLANES - 1, t_rows + CONV_HALO - SUBLANES, CONV_CW), F32),
                        pltpu.VMEM((t_rows, width), F32)],
        compiler_params=pltpu.CompilerParams(
            dimension_semantics=("arbitrary",),
            vmem_limit_bytes=_vmem_limit(blk + scratch)),
        name="conformer_conv",
    )(proj, proj, proj, proj, conv_w, row(conv_b), row(ln_g), row(ln_b))


HG_MXU_LEVELS = (3, 4)


def _decay_sum_matrix():
    c = CHUNK
    m = np.zeros((1 + len(HG_MXU_LEVELS), c, c), np.float32)
    for t in range(c):
        m[0, t, :t + 1] = 1.0
        for i, l in enumerate(HG_MXU_LEVELS):
            blk = c >> l
            mid = t - t % blk + blk // 2
            if t >= mid:
                m[1 + i, t, mid:t + 1] = 1.0
            else:
                m[1 + i, t, t + 1:mid] = 1.0
    m = m.reshape(-1, c)
    return np.concatenate([m, m], axis=1)


def _hgrn_kernel(q_ref, f_ref, v_ref, g_ref, lbl_ref, ng_ref, m_ref, o_ref, st_ref, *, tb, layer):
    t_blk = pl.program_id(1)

    @pl.when(t_blk == 0)
    def _():
        st_ref[...] = jnp.zeros_like(st_ref)

    c = CHUNK
    n_chunks = tb // c
    w = HG_PAIR * HG_DK
    heads = [slice(h * HG_DK, (h + 1) * HG_DK) for h in range(HG_PAIR)]
    lbl = lbl_ref[...]
    e = jnp.exp(lbl - jnp.max(lbl, axis=0, keepdims=True))
    lb = jnp.sum(e[:layer + 1], axis=0, keepdims=True) / jnp.sum(e, axis=0, keepdims=True)
    f_half = 0.5 * (1.0 - lb)
    f_mid = lb + f_half
    ng = ng_ref[...]
    m2 = m_ref[...]

    row = lax.broadcasted_iota(jnp.int32, (c, w), 0)
    upper_row = {l: (row % (c >> l)) >= (c >> (l + 1)) for l in range(3, N_LEVELS)}
    t_i = lax.broadcasted_iota(jnp.int32, (c, HG_PAIR * c), 0)
    s_i = lax.broadcasted_iota(jnp.int32, (c, HG_PAIR * c), 1) % c
    valid = []
    for l in range(N_LEVELS):
        blk = c >> l
        valid.append((t_i // blk == s_i // blk) & (t_i % blk >= blk // 2) & (s_i % blk < blk // 2))

    def blockdiag(a, zero):
        return jnp.concatenate([jnp.concatenate([a[:, heads[0]], zero], axis=1),
                                jnp.concatenate([zero, a[:, heads[1]]], axis=1)], axis=0)

    def per_head(fn):
        return jnp.concatenate([fn(hs) for hs in heads], axis=1)

    zero_c = jnp.zeros((c, HG_DK), BF16)
    zero_s = jnp.zeros((HG_DV, HG_DK), BF16)
    nt_dims = (((1,), (1,)), ((), ()))
    tn_dims = (((0,), (0,)), ((), ()))

    def level_operand(l, qf, kk, f, b_cum, ex):
        blk = c >> l
        half = blk // 2
        if blk >= 2 * SUBLANES:
            pieces, expo = [], []
            for s0 in range(0, c, blk):
                ref = jnp.broadcast_to(b_cum[s0 + half - 1:s0 + half, :], (half, w))
                lo = slice(s0, s0 + half)
                hi = slice(s0 + half, s0 + blk)
                pieces += [kk[lo, :], qf[hi, :]]
                expo += [ref - b_cum[lo, :], b_cum[hi, :] - ref]
            y = jnp.concatenate(pieces, axis=0) * jnp.exp2(jnp.concatenate(expo, axis=0))
        elif l in HG_MXU_LEVELS:
            i = HG_MXU_LEVELS.index(l)
            y = jnp.where(upper_row[l], qf, kk) * jnp.exp2(ex[(1 + i) * c:(2 + i) * c])
        else:
            y = jnp.where(upper_row[l], qf * f, kk)
        return y.astype(BF16)

    st = [st_ref[h] for h in range(HG_PAIR)]
    n_pairs = n_chunks // 2
    fronts, raws, updates, mids, st_ins = {}, {}, {}, {}, {}

    def stage_gates(p):
        out = []
        for ci in (2 * p, 2 * p + 1):
            rs = slice(ci * c, (ci + 1) * c)
            qf = _silu(q_ref[rs, :])
            f = f_mid + f_half * jnp.tanh(0.5 * f_ref[rs, :])
            log_f = jnp.log2(f)
            p0 = log_f.astype(BF16)
            p1 = (log_f - p0.astype(F32)).astype(BF16)
            ex = jnp.dot(m2, jnp.concatenate([p0, p1], axis=0), preferred_element_type=F32)
            out.append((rs, qf, 1.0 - f, f, ex[:c], ex))
        fronts[p] = out

    def stage_levels(p):
        raws[p] = []
        for l in range(N_LEVELS):
            ys = [level_operand(l, qf, kk, f, b_cum, ex) for (_, qf, kk, f, b_cum, ex) in fronts[p]]
            raws[p].append(lax.dot_general(jnp.concatenate(ys, axis=0),
                                           jnp.concatenate([blockdiag(y, zero_c) for y in ys], axis=0),
                                           nt_dims, preferred_element_type=F32))
        updates[p] = []
        for rs, qf, kk, f, b_cum, ex in fronts[p]:
            ke = (kk * jnp.exp2(b_cum[c - 1:c, :] - b_cum)).astype(BF16)
            updates[p].append(lax.dot_general(v_ref[rs, :].astype(BF16), ke, tn_dims,
                                              preferred_element_type=F32))

    def stage_intra(p):
        mids[p] = []
        for j, (rs, qf, kk, f, b_cum, ex) in enumerate(fronts[p]):
            scores = 0.0
            for l in range(N_LEVELS):
                s_l = raws[p][l][j * c:(j + 1) * c, j * HG_PAIR * c:(j + 1) * HG_PAIR * c]
                scores = jnp.where(valid[l], s_l, scores)
            v = v_ref[rs, :]
            o = jnp.dot(scores.astype(BF16), blockdiag(v.astype(BF16), zero_c),
                        preferred_element_type=F32)
            qk = qf * kk
            o = o + per_head(lambda hs: jnp.broadcast_to(
                jnp.sum(qk[:, hs], axis=-1, keepdims=True), (c, HG_DK))) * v
            qe = (qf * jnp.exp2(b_cum)).astype(BF16)
            mids[p].append((rs, o, qe, jnp.exp2(b_cum[c - 1:c, :])))

    def stage_state(p):
        st_ins[p] = []
        for (rs, o, qe, dec), ut in zip(mids[p], updates[p]):
            s0, s1 = (jnp.transpose(s).astype(BF16) for s in st)
            st_ins[p].append(jnp.concatenate([jnp.concatenate([s0, zero_s], axis=1),
                                              jnp.concatenate([zero_s, s1], axis=1)], axis=0))
            for h, hs in enumerate(heads):
                st[h] = st[h] * dec[:, hs] + ut[hs, hs]

    def stage_out(p):
        for (rs, o, qe, dec), st_bd in zip(mids[p], st_ins[p]):
            o = o + jnp.dot(qe, st_bd, preferred_element_type=F32)
            o2 = o * o
            inv = per_head(lambda hs: jnp.broadcast_to(
                lax.rsqrt(jnp.mean(o2[:, hs], axis=-1, keepdims=True) + EPS), (c, HG_DV)))
            o_ref[rs, :] = (o * inv * ng * _silu(g_ref[rs, :])).astype(o_ref.dtype)

    stages = (stage_gates, stage_levels, stage_intra, stage_state, stage_out)
    for tick in range(n_pairs + len(stages) - 1):
        for k, stage in enumerate(stages):
            if 0 <= tick - k < n_pairs:
                stage(tick - k)
    for h in range(HG_PAIR):
        st_ref[h] = st[h]


def _hgrn2(proj, lb_logits, norm_g, layer, col0, width, tb=2048):
    _, s, bw = proj.shape
    n_heads = width // HG_DK
    assert bw == HG_PAIR * HG_DK
    assert n_heads % HG_PAIR == 0 and s % tb == 0 and tb % CHUNK == 0 and col0 % bw == 0
    sec = lambda k: (lambda hg, t: ((col0 + k * width) // bw + hg, t, 0))
    depth1 = lb_logits.shape[0]
    m2 = jnp.asarray(_decay_sum_matrix(), BF16)
    blk = 4 * tb * bw * 4 + tb * bw * 2 + m2.size * 2
    return pl.pallas_call(
        functools.partial(_hgrn_kernel, tb=tb, layer=layer),
        grid=(n_heads // HG_PAIR, s // tb),
        in_specs=[pl.BlockSpec((None, tb, bw), sec(0)),
                  pl.BlockSpec((None, tb, bw), sec(1)),
                  pl.BlockSpec((None, tb, bw), sec(2)),
                  pl.BlockSpec((None, tb, bw), sec(3)),
                  pl.BlockSpec((depth1, bw), lambda hg, t: (0, hg)),
                  pl.BlockSpec((1, bw), lambda hg, t: (0, hg)),
                  pl.BlockSpec(m2.shape, lambda hg, t: (0, 0))],
        out_specs=pl.BlockSpec((tb, bw), lambda hg, t: (t, hg)),
        out_shape=jax.ShapeDtypeStruct((s, width), BF16),
        scratch_shapes=[pltpu.VMEM((HG_PAIR, HG_DV, HG_DK), F32)],
        compiler_params=pltpu.CompilerParams(
            dimension_semantics=("arbitrary", "arbitrary"),
            vmem_limit_bytes=_vmem_limit(blk + 8 * 1024 * 1024)),
        name="hgrn2",
    )(proj, proj, proj, proj, lb_logits, norm_g.reshape(1, width), m2)


def kernel(x, attn_norm_g, w_in, conv_w, conv_b, conv_ln_g, conv_ln_b, hg_lb_logits, hg_norm_g,
           w_out, ffn_norm_g, w_gate, w_up, w_down, final_norm_g):
    bsz, seq, d_model = x.shape
    assert bsz == 1
    depth = w_in.shape[0]
    conv_width = conv_w.shape[2]
    hg_width = hg_norm_g.shape[1]
    d_ff = w_gate.shape[2]
    bf = 256
    assert d_ff % bf == 0

    xs = x.reshape(seq, d_model)
    for l in range(depth):
        h = _rmsnorm(xs, attn_norm_g[l], BF16)
        proj, w_out_b, w_down_b = _matmul([h], w_in[l], None, F32, bm=2048, bn=512, lhs_bufs=1,
                                          side_casts=(w_out[l], w_down[l]), side_rows=128,
                                          out_col_tile=HG_PAIR * HG_DK)
        a_out = _conformer_conv(proj, conv_w[l], conv_b[l], conv_ln_g[l], conv_ln_b[l], conv_width)
        b_out = _hgrn2(proj, hg_lb_logits, hg_norm_g[l], l, 2 * conv_width, hg_width)
        xs = _matmul([a_out, b_out], w_out_b, xs, F32, bm=1024, bn=1024)
        h = _rmsnorm(xs, ffn_norm_g[l], BF16)
        u = _swiglu(h, w_gate[l], w_up[l], bm=2048, bf=bf)
        xs = _matmul([u], w_down_b, xs, F32, bm=512, bn=512)
    out = _rmsnorm(xs, final_norm_g, F32)
    return out.reshape(bsz, seq, d_model)
```

```python
import functools

import numpy as np
import jax
import jax.numpy as jnp
from jax import lax
from jax.experimental import pallas as pl
from jax.experimental.pallas import tpu as pltpu

F32 = jnp.float32
BF16 = jnp.bfloat16
EPS = 1e-6

LANES = 128
SUBLANES = 8
VMEM_BYTES = 64 * 1024 * 1024
VMEM_HEADROOM = 6 * 1024 * 1024

HG_DK = 128
HG_DV = 128
HG_PAIR = 2
CHUNK = 64
N_LEVELS = 6
CONV_K = 31
CONV_HALO = 32


def _vmem_limit(block_bytes):
    return int(min(VMEM_BYTES - 2 * 1024 * 1024, 2 * block_bytes + VMEM_HEADROOM))


def _sigmoid(x):
    return 1.0 / (1.0 + jnp.exp(-x))


def _silu(x):
    hx = 0.5 * x
    return hx + hx * jnp.tanh(hx)


def _rmsnorm_kernel(x_ref, g_ref, o_ref):
    x = x_ref[...]
    ms = jnp.mean(x * x, axis=-1, keepdims=True)
    o_ref[...] = (x * lax.rsqrt(ms + EPS) * g_ref[...]).astype(o_ref.dtype)


def _rmsnorm(x, g, out_dtype, bm=512):
    m, d = x.shape
    return pl.pallas_call(
        _rmsnorm_kernel,
        grid=(m // bm,),
        in_specs=[pl.BlockSpec((bm, d), lambda i: (i, 0)),
                  pl.BlockSpec((1, d), lambda i: (0, 0))],
        out_specs=pl.BlockSpec((bm, d), lambda i: (i, 0)),
        out_shape=jax.ShapeDtypeStruct((m, d), out_dtype),
        compiler_params=pltpu.CompilerParams(
            dimension_semantics=("arbitrary",),
            vmem_limit_bytes=_vmem_limit(bm * d * 10)),
        name="rmsnorm",
    )(x, g.reshape(1, d))


def _matmul_kernel(*refs, n_lhs, has_res, nk, side_blocks):
    lhs = refs[:n_lhs]
    ws = refs[n_lhs:2 * n_lhs]
    res_ref = refs[2 * n_lhs] if has_res else None
    n_in = 2 * n_lhs + has_res
    n_side = len(side_blocks)
    o_ref = refs[n_in + n_side]
    step = pl.program_id(0) * pl.num_programs(1) + pl.program_id(1)
    for s_in, s_out, n_blk in zip(refs[n_in:n_in + n_side], refs[n_in + n_side + 1:], side_blocks):
        @pl.when(step < n_blk)
        def _(s_in=s_in, s_out=s_out):
            s_out[...] = s_in[...].astype(s_out.dtype)
    acc = None
    for a_ref, w_ref in zip(lhs, ws):
        d = jnp.dot(a_ref[...], w_ref[...].astype(BF16), preferred_element_type=F32)
        acc = d if acc is None else acc + d
    if nk == 1:
        if has_res:
            acc = acc + res_ref[...]
        if len(o_ref.shape) == 3:
            ct = o_ref.shape[2]
            for t in range(o_ref.shape[0]):
                o_ref[t] = acc[:, t * ct:(t + 1) * ct].astype(o_ref.dtype)
        else:
            o_ref[...] = acc.astype(o_ref.dtype)
    else:
        k = pl.program_id(2)

        @pl.when(k == 0)
        def _():
            o_ref[...] = acc + res_ref[...] if has_res else acc

        @pl.when(k > 0)
        def _():
            o_ref[...] += acc


def _matmul(lhs_list, w, res, out_dtype, bm, bn, bk=None, lhs_bufs=2, side_casts=(), side_rows=256,
            out_col_tile=None):
    m, kdim = lhs_list[0].shape
    n = w.shape[1]
    bk = kdim if bk is None else bk
    nk = kdim // bk
    n_lhs = len(lhs_list)
    assert m % bm == 0 and n % bn == 0 and kdim % bk == 0 and w.shape[0] == n_lhs * kdim
    assert nk == 1 or out_dtype == F32
    w_bytes = jnp.dtype(w.dtype).itemsize
    w_spec = lambda p: pl.BlockSpec((bk, bn), lambda i, j, k: (p * nk + k, j))
    assert lhs_bufs == 2 or nk == 1
    in_specs = ([pl.BlockSpec((bm, bk), lambda i, j, k: (i, k), pipeline_mode=pl.Buffered(lhs_bufs))] * n_lhs
                + [w_spec(p) for p in range(n_lhs)])
    args = list(lhs_list) + [w] * n_lhs
    w_tmp = 1 if w.dtype != BF16 else 0
    blk = (n_lhs * (bm * bk * lhs_bufs + bk * bn * (w_bytes + w_tmp))
           + bm * bn * jnp.dtype(out_dtype).itemsize)
    if res is not None:
        in_specs.append(pl.BlockSpec((bm, bn), lambda i, j, k: (i, j)))
        args.append(res)
        blk += bm * bn * 4
    nj = n // bn
    if out_col_tile is None:
        out_specs = [pl.BlockSpec((bm, bn), lambda i, j, k: (i, j))]
        out_shape = [jax.ShapeDtypeStruct((m, n), out_dtype)]
    else:
        assert nk == 1 and bn % out_col_tile == 0
        out_specs = [pl.BlockSpec((bn // out_col_tile, bm, out_col_tile), lambda i, j, k: (j, i, 0))]
        out_shape = [jax.ShapeDtypeStruct((n // out_col_tile, m, out_col_tile), out_dtype)]
    for a in side_casts:
        rows, cols = a.shape
        n_blk = rows // side_rows
        assert nk == 1 and rows % side_rows == 0 and n_blk <= (m // bm) * nj
        side_map = lambda i, j, k, n_blk=n_blk: (jnp.minimum(i * nj + j, n_blk - 1), 0)
        in_specs.append(pl.BlockSpec((side_rows, cols), side_map))
        out_specs.append(pl.BlockSpec((side_rows, cols), side_map))
        out_shape.append(jax.ShapeDtypeStruct((rows, cols), BF16))
        args.append(a)
        blk += side_rows * cols * 6
    outs = pl.pallas_call(
        functools.partial(_matmul_kernel, n_lhs=n_lhs, has_res=res is not None, nk=nk,
                          side_blocks=tuple(a.shape[0] // side_rows for a in side_casts)),
        grid=(m // bm, nj, nk),
        in_specs=in_specs,
        out_specs=out_specs,
        out_shape=out_shape,
        compiler_params=pltpu.CompilerParams(
            dimension_semantics=("arbitrary", "arbitrary", "arbitrary"),
            vmem_limit_bytes=_vmem_limit(blk + bm * bn * 4)),
        name="matmul",
    )(*args)
    return outs if side_casts else outs[0]


SWIGLU_ROW_SPLITS = 4


def _swiglu_kernel(h_ref, wg_ref, wu_ref, u_ref):
    wg = wg_ref[...].astype(BF16)
    wu = wu_ref[...].astype(BF16)
    rows = h_ref.shape[0] // SWIGLU_ROW_SPLITS
    for r in range(SWIGLU_ROW_SPLITS):
        rs = slice(r * rows, (r + 1) * rows)
        h = h_ref[rs, :]
        g = jnp.dot(h, wg, preferred_element_type=F32)
        u = jnp.dot(h, wu, preferred_element_type=F32)
        u_ref[rs, :] = (_silu(g) * u).astype(u_ref.dtype)


def _swiglu(h, w_gate, w_up, bm, bf):
    m, d = h.shape
    d_ff = w_gate.shape[1]
    assert m % bm == 0 and d_ff % bf == 0
    w_bytes = jnp.dtype(w_gate.dtype).itemsize
    blk = bm * d + 2 * d * bf * (w_bytes + 1) + bm * bf * 2
    return pl.pallas_call(
        _swiglu_kernel,
        grid=(m // bm, d_ff // bf),
        in_specs=[pl.BlockSpec((bm, d), lambda i, j: (i, 0), pipeline_mode=pl.Buffered(1)),
                  pl.BlockSpec((d, bf), lambda i, j: (0, j)),
                  pl.BlockSpec((d, bf), lambda i, j: (0, j))],
        out_specs=pl.BlockSpec((bm, bf), lambda i, j: (i, j)),
        out_shape=jax.ShapeDtypeStruct((m, d_ff), BF16),
        compiler_params=pltpu.CompilerParams(
            dimension_semantics=("arbitrary", "arbitrary"),
            vmem_limit_bytes=_vmem_limit(blk + bm * 2 * bf * 4)),
        name="swiglu",
    )(h, w_gate, w_up)


CONV_CW = 128
CONV_R = 128


def _conv_kernel(xv_ref, xg_ref, hv_ref, hg_ref, w_ref, b_ref, lg_ref, lb_ref, o_ref,
                 hext_ref, sh_ref, y_ref, *, t_rows, width):
    i = pl.program_id(0)
    ct = xv_ref.shape[2]
    for t in range(xv_ref.shape[0]):
        ts = slice(t * ct, (t + 1) * ct)
        hext_ref[CONV_HALO:, ts] = xv_ref[t] * _sigmoid(xg_ref[t])
        halo = hv_ref[t] * _sigmoid(hg_ref[t])
        hext_ref[:CONV_HALO, ts] = jnp.where(i > 0, halo, 0.0)

    first = CONV_HALO - (CONV_K - 1)
    n_sh = t_rows + CONV_HALO - SUBLANES
    for c in range(width // CONV_CW):
        cs = slice(c * CONV_CW, (c + 1) * CONV_CW)
        for b in range(1, SUBLANES):
            sh_ref[b - 1] = hext_ref[pl.ds(b, n_sh), cs]

        def row_tile(r, carry, cs=cs):
            r0 = pl.multiple_of(r * CONV_R, CONV_R)
            acc = jnp.zeros((CONV_R, CONV_CW), F32)
            for k in range(CONV_K):
                a, b = divmod(first + k, SUBLANES)
                if b == 0:
                    src = hext_ref[pl.ds(r0 + SUBLANES * a, CONV_R), cs]
                else:
                    src = sh_ref[b - 1, pl.ds(r0 + SUBLANES * a, CONV_R), :]
                acc = acc + w_ref[k:k + 1, cs] * src
            y_ref[pl.ds(r0, CONV_R), cs] = acc + b_ref[:, cs]
            return carry

        lax.fori_loop(0, t_rows // CONV_R, row_tile, 0)

    y = y_ref[...]
    mu = jnp.mean(y, axis=-1, keepdims=True)
    yc = y - mu
    var = jnp.mean(yc * yc, axis=-1, keepdims=True)
    z = yc * lax.rsqrt(var + EPS) * lg_ref[...] + lb_ref[...]
    o_ref[...] = (z * _sigmoid(z)).astype(o_ref.dtype)


def _conformer_conv(proj, conv_w, conv_b, ln_g, ln_b, width, t_rows=256):
    _, s, ct = proj.shape
    nt = width // ct
    hb = t_rows // CONV_HALO
    row = lambda a: a.reshape(1, width)
    halo_map = lambda c: (lambda i: (c, jnp.maximum(i * hb - 1, 0), 0))
    blk = (2 * t_rows + 2 * CONV_HALO + 40) * width * 4 + t_rows * width * 2
    scratch = ((t_rows + CONV_HALO) * width + 7 * (t_rows + CONV_HALO) * CONV_CW + t_rows * width) * 4
    return pl.pallas_call(
        functools.partial(_conv_kernel, t_rows=t_rows, width=width),
        grid=(s // t_rows,),
        in_specs=[pl.BlockSpec((nt, t_rows, ct), lambda i: (0, i, 0)),
                  pl.BlockSpec((nt, t_rows, ct), lambda i: (1, i, 0)),
                  pl.BlockSpec((nt, CONV_HALO, ct), halo_map(0)),
                  pl.BlockSpec((nt, CONV_HALO, ct), halo_map(1)),
                  pl.BlockSpec((CONV_K, width), lambda i: (0, 0)),
                  pl.BlockSpec((1, width), lambda i: (0, 0)),
                  pl.BlockSpec((1, width), lambda i: (0, 0)),
                  pl.BlockSpec((1, width), lambda i: (0, 0))],
        out_specs=pl.BlockSpec((t_rows, width), lambda i: (i, 0)),
        out_shape=jax.ShapeDtypeStruct((s, width), BF16),
        scratch_shapes=[pltpu.VMEM((t_rows + CONV_HALO, width), F32),
                        pltpu.VMEM((SUBLANES - 1, t_rows + CONV_HALO - SUBLANES, CONV_CW), F32),
                        pltpu.VMEM((t_rows, width), F32)],
        compiler_params=pltpu.CompilerParams(
            dimension_semantics=("arbitrary",),
            vmem_limit_bytes=_vmem_limit(blk + scratch)),
        name="conformer_conv",
    )(proj, proj, proj, proj, conv_w, row(conv_b), row(ln_g), row(ln_b))


HG_MXU_LEVELS = (3, 4)


def _decay_sum_matrix():
    c = CHUNK
    m = np.zeros((1 + len(HG_MXU_LEVELS), c, c), np.float32)
    for t in range(c):
        m[0, t, :t + 1] = 1.0
        for i, l in enumerate(HG_MXU_LEVELS):
            blk = c >> l
            mid = t - t % blk + blk // 2
            if t >= mid:
                m[1 + i, t, mid:t + 1] = 1.0
            else:
                m[1 + i, t, t + 1:mid] = 1.0
    m = m.reshape(-1, c)
    return np.concatenate([m, m], axis=1)


def _hgrn_kernel(q_ref, f_ref, v_ref, g_ref, lbl_ref, ng_ref, m_ref, o_ref, st_ref, *, tb, layer):
    t_blk = pl.program_id(1)

    @pl.when(t_blk == 0)
    def _():
        st_ref[...] = jnp.zeros_like(st_ref)

    c = CHUNK
    n_chunks = tb // c
    w = HG_PAIR * HG_DK
    heads = [slice(h * HG_DK, (h + 1) * HG_DK) for h in range(HG_PAIR)]
    lbl = lbl_ref[...]
    e = jnp.exp(lbl - jnp.max(lbl, axis=0, keepdims=True))
    lb = jnp.sum(e[:layer + 1], axis=0, keepdims=True) / jnp.sum(e, axis=0, keepdims=True)
    f_half = 0.5 * (1.0 - lb)
    f_mid = lb + f_half
    ng = ng_ref[...]
    m2 = m_ref[...]

    row = lax.broadcasted_iota(jnp.int32, (c, w), 0)
    upper_row = {l: (row % (c >> l)) >= (c >> (l + 1)) for l in range(3, N_LEVELS)}
    t_i = lax.broadcasted_iota(jnp.int32, (c, HG_PAIR * c), 0)
    s_i = lax.broadcasted_iota(jnp.int32, (c, HG_PAIR * c), 1) % c
    valid = []
    for l in range(N_LEVELS):
        blk = c >> l
        valid.append((t_i // blk == s_i // blk) & (t_i % blk >= blk // 2) & (s_i % blk < blk // 2))

    def blockdiag(a, zero):
        return jnp.concatenate([jnp.concatenate([a[:, heads[0]], zero], axis=1),
                                jnp.concatenate([zero, a[:, heads[1]]], axis=1)], axis=0)

    def per_head(fn):
        return jnp.concatenate([fn(hs) for hs in heads], axis=1)

    zero_c = jnp.zeros((c, HG_DK), BF16)
    zero_s = jnp.zeros((HG_DV, HG_DK), BF16)
    nt_dims = (((1,), (1,)), ((), ()))
    tn_dims = (((0,), (0,)), ((), ()))

    def level_operand(l, qf, kk, f, b_cum, ex):
        blk = c >> l
        half = blk // 2
        if blk >= 2 * SUBLANES:
            pieces, expo = [], []
            for s0 in range(0, c, blk):
                ref = jnp.broadcast_to(b_cum[s0 + half - 1:s0 + half, :], (half, w))
                lo = slice(s0, s0 + half)
                hi = slice(s0 + half, s0 + blk)
                pieces += [kk[lo, :], qf[hi, :]]
                expo += [ref - b_cum[lo, :], b_cum[hi, :] - ref]
            y = jnp.concatenate(pieces, axis=0) * jnp.exp2(jnp.concatenate(expo, axis=0))
        elif l in HG_MXU_LEVELS:
            i = HG_MXU_LEVELS.index(l)
            y = jnp.where(upper_row[l], qf, kk) * jnp.exp2(ex[(1 + i) * c:(2 + i) * c])
        else:
            y = jnp.where(upper_row[l], qf * f, kk)
        return y.astype(BF16)

    st = [st_ref[h] for h in range(HG_PAIR)]
    n_pairs = n_chunks // 2
    fronts, raws, updates, mids, st_ins = {}, {}, {}, {}, {}

    def stage_gates(p):
        out = []
        for ci in (2 * p, 2 * p + 1):
            rs = slice(ci * c, (ci + 1) * c)
            qf = _silu(q_ref[rs, :])
            f = f_mid + f_half * jnp.tanh(0.5 * f_ref[rs, :])
            log_f = jnp.log2(f)
            p0 = log_f.astype(BF16)
            p1 = (log_f - p0.astype(F32)).astype(BF16)
            ex = jnp.dot(m2, jnp.concatenate([p0, p1], axis=0), preferred_element_type=F32)
            out.append((rs, qf, 1.0 - f, f, ex[:c], ex))
        fronts[p] = out

    def stage_levels(p):
        raws[p] = []
        for l in range(N_LEVELS):
            ys = [level_operand(l, qf, kk, f, b_cum, ex) for (_, qf, kk, f, b_cum, ex) in fronts[p]]
            raws[p].append(lax.dot_general(jnp.concatenate(ys, axis=0),
                                           jnp.concatenate([blockdiag(y, zero_c) for y in ys], axis=0),
                                           nt_dims, preferred_element_type=F32))
        updates[p] = []
        for rs, qf, kk, f, b_cum, ex in fronts[p]:
            ke = (kk * jnp.exp2(b_cum[c - 1:c, :] - b_cum)).astype(BF16)
            updates[p].append(lax.dot_general(v_ref[rs, :].astype(BF16), ke, tn_dims,
                                              preferred_element_type=F32))

    def stage_intra(p):
        mids[p] = []
        for j, (rs, qf, kk, f, b_cum, ex) in enumerate(fronts[p]):
            scores = 0.0
            for l in range(N_LEVELS):
                s_l = raws[p][l][j * c:(j + 1) * c, j * HG_PAIR * c:(j + 1) * HG_PAIR * c]
                scores = jnp.where(valid[l], s_l, scores)
            v = v_ref[rs, :]
            o = jnp.dot(scores.astype(BF16), blockdiag(v.astype(BF16), zero_c),
                        preferred_element_type=F32)
            qk = qf * kk
            o = o + per_head(lambda hs: jnp.broadcast_to(
                jnp.sum(qk[:, hs], axis=-1, keepdims=True), (c, HG_DK))) * v
            qe = (qf * jnp.exp2(b_cum)).astype(BF16)
            mids[p].append((rs, o, qe, jnp.exp2(b_cum[c - 1:c, :])))

    def stage_state(p):
        st_ins[p] = []
        for (rs, o, qe, dec), ut in zip(mids[p], updates[p]):
            s0, s1 = (jnp.transpose(s).astype(BF16) for s in st)
            st_ins[p].append(jnp.concatenate([jnp.concatenate([s0, zero_s], axis=1),
                                              jnp.concatenate([zero_s, s1], axis=1)], axis=0))
            for h, hs in enumerate(heads):
                st[h] = st[h] * dec[:, hs] + ut[hs, hs]

    def stage_out(p):
        for (rs, o, qe, dec), st_bd in zip(mids[p], st_ins[p]):
            o = o + jnp.dot(qe, st_bd, preferred_element_type=F32)
            o2 = o * o
            inv = per_head(lambda hs: jnp.broadcast_to(
                lax.rsqrt(jnp.mean(o2[:, hs], axis=-1, keepdims=True) + EPS), (c, HG_DV)))
            o_ref[rs, :] = (o * inv * ng * _silu(g_ref[rs, :])).astype(o_ref.dtype)

    stages = (stage_gates, stage_levels, stage_intra, stage_state, stage_out)
    for tick in range(n_pairs + len(stages) - 1):
        for k, stage in enumerate(stages):
            if 0 <= tick - k < n_pairs:
                stage(tick - k)
    for h in range(HG_PAIR):
        st_ref[h] = st[h]


def _hgrn2(proj, lb_logits, norm_g, layer, col0, width, tb=2048):
    _, s, bw = proj.shape
    n_heads = width // HG_DK
    assert bw == HG_PAIR * HG_DK
    assert n_heads % HG_PAIR == 0 and s % tb == 0 and tb % CHUNK == 0 and col0 % bw == 0
    sec = lambda k: (lambda hg, t: ((col0 + k * width) // bw + hg, t, 0))
    depth1 = lb_logits.shape[0]
    m2 = jnp.asarray(_decay_sum_matrix(), BF16)
    blk = 4 * tb * bw * 4 + tb * bw * 2 + m2.size * 2
    return pl.pallas_call(
        functools.partial(_hgrn_kernel, tb=tb, layer=layer),
        grid=(n_heads // HG_PAIR, s // tb),
        in_specs=[pl.BlockSpec((None, tb, bw), sec(0)),
                  pl.BlockSpec((None, tb, bw), sec(1)),
                  pl.BlockSpec((None, tb, bw), sec(2)),
                  pl.BlockSpec((None, tb, bw), sec(3)),
                  pl.BlockSpec((depth1, bw), lambda hg, t: (0, hg)),
                  pl.BlockSpec((1, bw), lambda hg, t: (0, hg)),
                  pl.BlockSpec(m2.shape, lambda hg, t: (0, 0))],
        out_specs=pl.BlockSpec((tb, bw), lambda hg, t: (t, hg)),
        out_shape=jax.ShapeDtypeStruct((s, width), BF16),
        scratch_shapes=[pltpu.VMEM((HG_PAIR, HG_DV, HG_DK), F32)],
        compiler_params=pltpu.CompilerParams(
            dimension_semantics=("arbitrary", "arbitrary"),
            vmem_limit_bytes=_vmem_limit(blk + 8 * 1024 * 1024)),
        name="hgrn2",
    )(proj, proj, proj, proj, lb_logits, norm_g.reshape(1, width), m2)


def kernel(x, attn_norm_g, w_in, conv_w, conv_b, conv_ln_g, conv_ln_b, hg_lb_logits, hg_norm_g,
           w_out, ffn_norm_g, w_gate, w_up, w_down, final_norm_g):
    bsz, seq, d_model = x.shape
    assert bsz == 1
    depth = w_in.shape[0]
    conv_width = conv_w.shape[2]
    hg_width = hg_norm_g.shape[1]
    d_ff = w_gate.shape[2]
    bf = 256
    assert d_ff % bf == 0

    xs = x.reshape(seq, d_model)
    for l in range(depth):
        h = _rmsnorm(xs, attn_norm_g[l], BF16)
        proj, w_out_b, w_down_b = _matmul([h], w_in[l], None, F32, bm=2048, bn=512, lhs_bufs=1,
                                          side_casts=(w_out[l], w_down[l]), side_rows=128,
                                          out_col_tile=HG_PAIR * HG_DK)
        a_out = _conformer_conv(proj, conv_w[l], conv_b[l], conv_ln_g[l], conv_ln_b[l], conv_width)
        b_out = _hgrn2(proj, hg_lb_logits, hg_norm_g[l], l, 2 * conv_width, hg_width)
        xs = _matmul([a_out, b_out], w_out_b, xs, F32, bm=1024, bn=1024)
        h = _rmsnorm(xs, ffn_norm_g[l], BF16)
        u = _swiglu(h, w_gate[l], w_up[l], bm=4096, bf=bf)
        xs = _matmul([u], w_down_b, xs, F32, bm=512, bn=512)
    out = _rmsnorm(xs, final_norm_g, F32)
    return out.reshape(bsz, seq, d_model)
```

```python
import functools

import numpy as np
import jax
import jax.numpy as jnp
from jax import lax
from jax.experimental import pallas as pl
from jax.experimental.pallas import tpu as pltpu

F32 = jnp.float32
BF16 = jnp.bfloat16
EPS = 1e-6

LANES = 128
SUBLANES = 8
VMEM_BYTES = 64 * 1024 * 1024
VMEM_HEADROOM = 6 * 1024 * 1024

HG_DK = 128
HG_DV = 128
HG_PAIR = 2
CHUNK = 64
N_LEVELS = 6
CONV_K = 31
CONV_HALO = 32


def _vmem_limit(block_bytes):
    return int(min(VMEM_BYTES - 2 * 1024 * 1024, 2 * block_bytes + VMEM_HEADROOM))


def _sigmoid(x):
    return 1.0 / (1.0 + jnp.exp(-x))


def _silu(x):
    hx = 0.5 * x
    return hx + hx * jnp.tanh(hx)


def _rmsnorm_kernel(x_ref, g_ref, o_ref):
    x = x_ref[...]
    ms = jnp.mean(x * x, axis=-1, keepdims=True)
    o_ref[...] = (x * lax.rsqrt(ms + EPS) * g_ref[...]).astype(o_ref.dtype)


def _rmsnorm(x, g, out_dtype, bm=512):
    m, d = x.shape
    return pl.pallas_call(
        _rmsnorm_kernel,
        grid=(m // bm,),
        in_specs=[pl.BlockSpec((bm, d), lambda i: (i, 0)),
                  pl.BlockSpec((1, d), lambda i: (0, 0))],
        out_specs=pl.BlockSpec((bm, d), lambda i: (i, 0)),
        out_shape=jax.ShapeDtypeStruct((m, d), out_dtype),
        compiler_params=pltpu.CompilerParams(
            dimension_semantics=("arbitrary",),
            vmem_limit_bytes=_vmem_limit(bm * d * 10)),
        name="rmsnorm",
    )(x, g.reshape(1, d))


def _matmul_kernel(*refs, n_lhs, has_res, nk, side_blocks):
    lhs = refs[:n_lhs]
    ws = refs[n_lhs:2 * n_lhs]
    res_ref = refs[2 * n_lhs] if has_res else None
    n_in = 2 * n_lhs + has_res
    n_side = len(side_blocks)
    o_ref = refs[n_in + n_side]
    step = pl.program_id(0) * pl.num_programs(1) + pl.program_id(1)
    for s_in, s_out, n_blk in zip(refs[n_in:n_in + n_side], refs[n_in + n_side + 1:], side_blocks):
        @pl.when(step < n_blk)
        def _(s_in=s_in, s_out=s_out):
            s_out[...] = s_in[...].astype(s_out.dtype)
    acc = None
    for a_ref, w_ref in zip(lhs, ws):
        d = jnp.dot(a_ref[...], w_ref[...].astype(BF16), preferred_element_type=F32)
        acc = d if acc is None else acc + d
    if nk == 1:
        if has_res:
            acc = acc + res_ref[...]
        if len(o_ref.shape) == 3:
            ct = o_ref.shape[2]
            for t in range(o_ref.shape[0]):
                o_ref[t] = acc[:, t * ct:(t + 1) * ct].astype(o_ref.dtype)
        else:
            o_ref[...] = acc.astype(o_ref.dtype)
    else:
        k = pl.program_id(2)

        @pl.when(k == 0)
        def _():
            o_ref[...] = acc + res_ref[...] if has_res else acc

        @pl.when(k > 0)
        def _():
            o_ref[...] += acc


def _matmul(lhs_list, w, res, out_dtype, bm, bn, bk=None, lhs_bufs=2, side_casts=(), side_rows=256,
            out_col_tile=None):
    m, kdim = lhs_list[0].shape
    n = w.shape[1]
    bk = kdim if bk is None else bk
    nk = kdim // bk
    n_lhs = len(lhs_list)
    assert m % bm == 0 and n % bn == 0 and kdim % bk == 0 and w.shape[0] == n_lhs * kdim
    assert nk == 1 or out_dtype == F32
    w_bytes = jnp.dtype(w.dtype).itemsize
    w_spec = lambda p: pl.BlockSpec((bk, bn), lambda i, j, k: (p * nk + k, j))
    assert lhs_bufs == 2 or nk == 1
    in_specs = ([pl.BlockSpec((bm, bk), lambda i, j, k: (i, k), pipeline_mode=pl.Buffered(lhs_bufs))] * n_lhs
                + [w_spec(p) for p in range(n_lhs)])
    args = list(lhs_list) + [w] * n_lhs
    w_tmp = 1 if w.dtype != BF16 else 0
    blk = (n_lhs * (bm * bk * lhs_bufs + bk * bn * (w_bytes + w_tmp))
           + bm * bn * jnp.dtype(out_dtype).itemsize)
    if res is not None:
        in_specs.append(pl.BlockSpec((bm, bn), lambda i, j, k: (i, j)))
        args.append(res)
        blk += bm * bn * 4
    nj = n // bn
    if out_col_tile is None:
        out_specs = [pl.BlockSpec((bm, bn), lambda i, j, k: (i, j))]
        out_shape = [jax.ShapeDtypeStruct((m, n), out_dtype)]
    else:
        assert nk == 1 and bn % out_col_tile == 0
        out_specs = [pl.BlockSpec((bn // out_col_tile, bm, out_col_tile), lambda i, j, k: (j, i, 0))]
        out_shape = [jax.ShapeDtypeStruct((n // out_col_tile, m, out_col_tile), out_dtype)]
    for a in side_casts:
        rows, cols = a.shape
        n_blk = rows // side_rows
        assert nk == 1 and rows % side_rows == 0 and n_blk <= (m // bm) * nj
        side_map = lambda i, j, k, n_blk=n_blk: (jnp.minimum(i * nj + j, n_blk - 1), 0)
        in_specs.append(pl.BlockSpec((side_rows, cols), side_map))
        out_specs.append(pl.BlockSpec((side_rows, cols), side_map))
        out_shape.append(jax.ShapeDtypeStruct((rows, cols), BF16))
        args.append(a)
        blk += side_rows * cols * 6
    outs = pl.pallas_call(
        functools.partial(_matmul_kernel, n_lhs=n_lhs, has_res=res is not None, nk=nk,
                          side_blocks=tuple(a.shape[0] // side_rows for a in side_casts)),
        grid=(m // bm, nj, nk),
        in_specs=in_specs,
        out_specs=out_specs,
        out_shape=out_shape,
        compiler_params=pltpu.CompilerParams(
            dimension_semantics=("arbitrary", "arbitrary", "arbitrary"),
            vmem_limit_bytes=_vmem_limit(blk + bm * bn * 4)),
        name="matmul",
    )(*args)
    return outs if side_casts else outs[0]


def _outproj_kernel(a_ref, b_ref, wa_ref, wb_ref, res_ref, g_ref, x1_ref, hp_ref, ssq_ref):
    j = pl.program_id(1)
    x1 = (jnp.dot(a_ref[...], wa_ref[...], preferred_element_type=F32)
          + jnp.dot(b_ref[...], wb_ref[...], preferred_element_type=F32)) + res_ref[...]
    x1_ref[...] = x1
    hp_ref[...] = (x1 * g_ref[...]).astype(hp_ref.dtype)
    sq = x1 * x1
    part = sq[:, :LANES]
    for t in range(1, sq.shape[1] // LANES):
        part = part + sq[:, t * LANES:(t + 1) * LANES]

    @pl.when(j == 0)
    def _():
        ssq_ref[...] = part

    @pl.when(j > 0)
    def _():
        ssq_ref[...] += part


def _outproj_norm(a, b, w, res, g, bm, bn):
    m, ka = a.shape
    n = w.shape[1]
    assert b.shape == (m, ka) and w.shape[0] == 2 * ka and m % bm == 0 and n % bn == 0
    blk = 2 * bm * ka * 2 + 2 * ka * bn * 2 + bm * bn * (4 + 4 + 2) + bm * LANES * 4
    return pl.pallas_call(
        _outproj_kernel,
        grid=(m // bm, n // bn),
        in_specs=[pl.BlockSpec((bm, ka), lambda i, j: (i, 0)),
                  pl.BlockSpec((bm, ka), lambda i, j: (i, 0)),
                  pl.BlockSpec((ka, bn), lambda i, j: (0, j)),
                  pl.BlockSpec((ka, bn), lambda i, j: (1, j)),
                  pl.BlockSpec((bm, bn), lambda i, j: (i, j)),
                  pl.BlockSpec((1, bn), lambda i, j: (0, j))],
        out_specs=[pl.BlockSpec((bm, bn), lambda i, j: (i, j)),
                   pl.BlockSpec((bm, bn), lambda i, j: (i, j)),
                   pl.BlockSpec((bm, LANES), lambda i, j: (i, 0))],
        out_shape=[jax.ShapeDtypeStruct((m, n), F32),
                   jax.ShapeDtypeStruct((m, n), BF16),
                   jax.ShapeDtypeStruct((m, LANES), F32)],
        compiler_params=pltpu.CompilerParams(
            dimension_semantics=("arbitrary", "arbitrary"),
            vmem_limit_bytes=_vmem_limit(blk + bm * bn * 4)),
        name="outproj_norm",
    )(a, b, w, w, res, g.reshape(1, n))


SWIGLU_ROW_SPLITS = 4


def _swiglu_kernel(h_ref, ssq_ref, wg_ref, wu_ref, u_ref):
    wg = wg_ref[...].astype(BF16)
    wu = wu_ref[...].astype(BF16)
    rows = h_ref.shape[0] // SWIGLU_ROW_SPLITS
    d = h_ref.shape[1]
    for r in range(SWIGLU_ROW_SPLITS):
        rs = slice(r * rows, (r + 1) * rows)
        h = h_ref[rs, :]
        inv = lax.rsqrt(jnp.sum(ssq_ref[rs, :], axis=-1, keepdims=True) / d + EPS)
        g = jnp.dot(h, wg, preferred_element_type=F32) * inv
        u = jnp.dot(h, wu, preferred_element_type=F32) * inv
        u_ref[rs, :] = (_silu(g) * u).astype(u_ref.dtype)


def _swiglu(h, ssq, w_gate, w_up, bm, bf):
    m, d = h.shape
    d_ff = w_gate.shape[1]
    assert m % bm == 0 and d_ff % bf == 0 and ssq.shape == (m, LANES)
    w_bytes = jnp.dtype(w_gate.dtype).itemsize
    blk = bm * d + 2 * d * bf * (w_bytes + 1) + bm * bf * 2 + bm * LANES * 4
    return pl.pallas_call(
        _swiglu_kernel,
        grid=(m // bm, d_ff // bf),
        in_specs=[pl.BlockSpec((bm, d), lambda i, j: (i, 0), pipeline_mode=pl.Buffered(1)),
                  pl.BlockSpec((bm, LANES), lambda i, j: (i, 0)),
                  pl.BlockSpec((d, bf), lambda i, j: (0, j)),
                  pl.BlockSpec((d, bf), lambda i, j: (0, j))],
        out_specs=pl.BlockSpec((bm, bf), lambda i, j: (i, j)),
        out_shape=jax.ShapeDtypeStruct((m, d_ff), BF16),
        compiler_params=pltpu.CompilerParams(
            dimension_semantics=("arbitrary", "arbitrary"),
            vmem_limit_bytes=_vmem_limit(blk + bm * 2 * bf * 4)),
        name="swiglu",
    )(h, ssq, w_gate, w_up)


CONV_CW = 128
CONV_R = 128


def _conv_kernel(xv_ref, xg_ref, hv_ref, hg_ref, w_ref, b_ref, lg_ref, lb_ref, o_ref,
                 hext_ref, sh_ref, y_ref, *, t_rows, width):
    i = pl.program_id(0)
    ct = xv_ref.shape[2]
    for t in range(xv_ref.shape[0]):
        ts = slice(t * ct, (t + 1) * ct)
        hext_ref[CONV_HALO:, ts] = xv_ref[t] * _sigmoid(xg_ref[t])
        halo = hv_ref[t] * _sigmoid(hg_ref[t])
        hext_ref[:CONV_HALO, ts] = jnp.where(i > 0, halo, 0.0)

    first = CONV_HALO - (CONV_K - 1)
    n_sh = t_rows + CONV_HALO - SUBLANES
    for c in range(width // CONV_CW):
        cs = slice(c * CONV_CW, (c + 1) * CONV_CW)
        for b in range(1, SUBLANES):
            sh_ref[b - 1] = hext_ref[pl.ds(b, n_sh), cs]

        def row_tile(r, carry, cs=cs):
            r0 = pl.multiple_of(r * CONV_R, CONV_R)
            acc = jnp.zeros((CONV_R, CONV_CW), F32)
            for k in range(CONV_K):
                a, b = divmod(first + k, SUBLANES)
                if b == 0:
                    src = hext_ref[pl.ds(r0 + SUBLANES * a, CONV_R), cs]
                else:
                    src = sh_ref[b - 1, pl.ds(r0 + SUBLANES * a, CONV_R), :]
                acc = acc + w_ref[k:k + 1, cs] * src
            y_ref[pl.ds(r0, CONV_R), cs] = acc + b_ref[:, cs]
            return carry

        lax.fori_loop(0, t_rows // CONV_R, row_tile, 0)

    y = y_ref[...]
    mu = jnp.mean(y, axis=-1, keepdims=True)
    yc = y - mu
    var = jnp.mean(yc * yc, axis=-1, keepdims=True)
    z = yc * lax.rsqrt(var + EPS) * lg_ref[...] + lb_ref[...]
    o_ref[...] = (z * _sigmoid(z)).astype(o_ref.dtype)


def _conformer_conv(proj, conv_w, conv_b, ln_g, ln_b, width, t_rows=256):
    _, s, ct = proj.shape
    nt = width // ct
    hb = t_rows // CONV_HALO
    row = lambda a: a.reshape(1, width)
    halo_map = lambda c: (lambda i: (c, jnp.maximum(i * hb - 1, 0), 0))
    blk = (2 * t_rows + 2 * CONV_HALO + 40) * width * 4 + t_rows * width * 2
    scratch = ((t_rows + CONV_HALO) * width + 7 * (t_rows + CONV_HALO) * CONV_CW + t_rows * width) * 4
    return pl.pallas_call(
        functools.partial(_conv_kernel, t_rows=t_rows, width=width),
        grid=(s // t_rows,),
        in_specs=[pl.BlockSpec((nt, t_rows, ct), lambda i: (0, i, 0)),
                  pl.BlockSpec((nt, t_rows, ct), lambda i: (1, i, 0)),
                  pl.BlockSpec((nt, CONV_HALO, ct), halo_map(0)),
                  pl.BlockSpec((nt, CONV_HALO, ct), halo_map(1)),
                  pl.BlockSpec((CONV_K, width), lambda i: (0, 0)),
                  pl.BlockSpec((1, width), lambda i: (0, 0)),
                  pl.BlockSpec((1, width), lambda i: (0, 0)),
                  pl.BlockSpec((1, width), lambda i: (0, 0))],
        out_specs=pl.BlockSpec((t_rows, width), lambda i: (i, 0)),
        out_shape=jax.ShapeDtypeStruct((s, width), BF16),
        scratch_shapes=[pltpu.VMEM((t_rows + CONV_HALO, width), F32),
                        pltpu.VMEM((SUBLANES - 1, t_rows + CONV_HALO - SUBLANES, CONV_CW), F32),
                        pltpu.VMEM((t_rows, width), F32)],
        compiler_params=pltpu.CompilerParams(
            dimension_semantics=("arbitrary",),
            vmem_limit_bytes=_vmem_limit(blk + scratch)),
        name="conformer_conv",
    )(proj, proj, proj, proj, conv_w, row(conv_b), row(ln_g), row(ln_b))


HG_MXU_LEVELS = (3, 4)


def _decay_sum_matrix():
    c = CHUNK
    m = np.zeros((1 + len(HG_MXU_LEVELS), c, c), np.float32)
    for t in range(c):
        m[0, t, :t + 1] = 1.0
        for i, l in enumerate(HG_MXU_LEVELS):
            blk = c >> l
            mid = t - t % blk + blk // 2
            if t >= mid:
                m[1 + i, t, mid:t + 1] = 1.0
            else:
                m[1 + i, t, t + 1:mid] = 1.0
    m = m.reshape(-1, c)
    return np.concatenate([m, m], axis=1)


def _hgrn_kernel(q_ref, f_ref, v_ref, g_ref, lbl_ref, ng_ref, m_ref, o_ref, st_ref, *, tb, layer):
    t_blk = pl.program_id(1)

    @pl.when(t_blk == 0)
    def _():
        st_ref[...] = jnp.zeros_like(st_ref)

    c = CHUNK
    n_chunks = tb // c
    w = HG_PAIR * HG_DK
    heads = [slice(h * HG_DK, (h + 1) * HG_DK) for h in range(HG_PAIR)]
    lbl = lbl_ref[...]
    e = jnp.exp(lbl - jnp.max(lbl, axis=0, keepdims=True))
    lb = jnp.sum(e[:layer + 1], axis=0, keepdims=True) / jnp.sum(e, axis=0, keepdims=True)
    f_half = 0.5 * (1.0 - lb)
    f_mid = lb + f_half
    ng = ng_ref[...]
    m2 = m_ref[...]

    row = lax.broadcasted_iota(jnp.int32, (c, w), 0)
    upper_row = {l: (row % (c >> l)) >= (c >> (l + 1)) for l in range(3, N_LEVELS)}
    t_i = lax.broadcasted_iota(jnp.int32, (c, HG_PAIR * c), 0)
    s_i = lax.broadcasted_iota(jnp.int32, (c, HG_PAIR * c), 1) % c
    valid = []
    for l in range(N_LEVELS):
        blk = c >> l
        valid.append((t_i // blk == s_i // blk) & (t_i % blk >= blk // 2) & (s_i % blk < blk // 2))

    def blockdiag(a, zero):
        return jnp.concatenate([jnp.concatenate([a[:, heads[0]], zero], axis=1),
                                jnp.concatenate([zero, a[:, heads[1]]], axis=1)], axis=0)

    def per_head(fn):
        return jnp.concatenate([fn(hs) for hs in heads], axis=1)

    zero_c = jnp.zeros((c, HG_DK), BF16)
    zero_s = jnp.zeros((HG_DV, HG_DK), BF16)
    nt_dims = (((1,), (1,)), ((), ()))
    tn_dims = (((0,), (0,)), ((), ()))

    def level_operand(l, qf, kk, f, b_cum, ex):
        blk = c >> l
        half = blk // 2
        if blk >= 2 * SUBLANES:
            pieces, expo = [], []
            for s0 in range(0, c, blk):
                ref = jnp.broadcast_to(b_cum[s0 + half - 1:s0 + half, :], (half, w))
                lo = slice(s0, s0 + half)
                hi = slice(s0 + half, s0 + blk)
                pieces += [kk[lo, :], qf[hi, :]]
                expo += [ref - b_cum[lo, :], b_cum[hi, :] - ref]
            y = jnp.concatenate(pieces, axis=0) * jnp.exp2(jnp.concatenate(expo, axis=0))
        elif l in HG_MXU_LEVELS:
            i = HG_MXU_LEVELS.index(l)
            y = jnp.where(upper_row[l], qf, kk) * jnp.exp2(ex[(1 + i) * c:(2 + i) * c])
        else:
            y = jnp.where(upper_row[l], qf * f, kk)
        return y.astype(BF16)

    st = [st_ref[h] for h in range(HG_PAIR)]
    n_pairs = n_chunks // 2
    fronts, raws, updates, mids, st_ins = {}, {}, {}, {}, {}

    def stage_gates(p):
        out = []
        for ci in (2 * p, 2 * p + 1):
            rs = slice(ci * c, (ci + 1) * c)
            qf = _silu(q_ref[rs, :])
            f = f_mid + f_half * jnp.tanh(0.5 * f_ref[rs, :])
            log_f = jnp.log2(f)
            p0 = log_f.astype(BF16)
            p1 = (log_f - p0.astype(F32)).astype(BF16)
            ex = jnp.dot(m2, jnp.concatenate([p0, p1], axis=0), preferred_element_type=F32)
            out.append((rs, qf, 1.0 - f, f, ex[:c], ex))
        fronts[p] = out

    def stage_levels(p):
        raws[p] = []
        for l in range(N_LEVELS):
            ys = [level_operand(l, qf, kk, f, b_cum, ex) for (_, qf, kk, f, b_cum, ex) in fronts[p]]
            raws[p].append(lax.dot_general(jnp.concatenate(ys, axis=0),
                                           jnp.concatenate([blockdiag(y, zero_c) for y in ys], axis=0),
                                           nt_dims, preferred_element_type=F32))
        updates[p] = []
        for rs, qf, kk, f, b_cum, ex in fronts[p]:
            ke = (kk * jnp.exp2(b_cum[c - 1:c, :] - b_cum)).astype(BF16)
            updates[p].append(lax.dot_general(v_ref[rs, :].astype(BF16), ke, tn_dims,
                                              preferred_element_type=F32))

    def stage_intra(p):
        mids[p] = []
        for j, (rs, qf, kk, f, b_cum, ex) in enumerate(fronts[p]):
            scores = 0.0
            for l in range(N_LEVELS):
                s_l = raws[p][l][j * c:(j + 1) * c, j * HG_PAIR * c:(j + 1) * HG_PAIR * c]
                scores = jnp.where(valid[l], s_l, scores)
            v = v_ref[rs, :]
            o = jnp.dot(scores.astype(BF16), blockdiag(v.astype(BF16), zero_c),
                        preferred_element_type=F32)
            qk = qf * kk
            o = o + per_head(lambda hs: jnp.broadcast_to(
                jnp.sum(qk[:, hs], axis=-1, keepdims=True), (c, HG_DK))) * v
            qe = (qf * jnp.exp2(b_cum)).astype(BF16)
            mids[p].append((rs, o, qe, jnp.exp2(b_cum[c - 1:c, :])))

    def stage_state(p):
        st_ins[p] = []
        for (rs, o, qe, dec), ut in zip(mids[p], updates[p]):
            s0, s1 = (jnp.transpose(s).astype(BF16) for s in st)
            st_ins[p].append(jnp.concatenate([jnp.concatenate([s0, zero_s], axis=1),
                                              jnp.concatenate([zero_s, s1], axis=1)], axis=0))
            for h, hs in enumerate(heads):
                st[h] = st[h] * dec[:, hs] + ut[hs, hs]

    def stage_out(p):
        for (rs, o, qe, dec), st_bd in zip(mids[p], st_ins[p]):
            o = o + jnp.dot(qe, st_bd, preferred_element_type=F32)
            o2 = o * o
            inv = per_head(lambda hs: jnp.broadcast_to(
                lax.rsqrt(jnp.mean(o2[:, hs], axis=-1, keepdims=True) + EPS), (c, HG_DV)))
            o_ref[rs, :] = (o * inv * ng * _silu(g_ref[rs, :])).astype(o_ref.dtype)

    stages = (stage_gates, stage_levels, stage_intra, stage_state, stage_out)
    for tick in range(n_pairs + len(stages) - 1):
        for k, stage in enumerate(stages):
            if 0 <= tick - k < n_pairs:
                stage(tick - k)
    for h in range(HG_PAIR):
        st_ref[h] = st[h]


def _hgrn2(proj, lb_logits, norm_g, layer, col0, width, tb=2048):
    _, s, bw = proj.shape
    n_heads = width // HG_DK
    assert bw == HG_PAIR * HG_DK
    assert n_heads % HG_PAIR == 0 and s % tb == 0 and tb % CHUNK == 0 and col0 % bw == 0
    sec = lambda k: (lambda hg, t: ((col0 + k * width) // bw + hg, t, 0))
    depth1 = lb_logits.shape[0]
    m2 = jnp.asarray(_decay_sum_matrix(), BF16)
    blk = 4 * tb * bw * 4 + tb * bw * 2 + m2.size * 2
    return pl.pallas_call(
        functools.partial(_hgrn_kernel, tb=tb, layer=layer),
        grid=(n_heads // HG_PAIR, s // tb),
        in_specs=[pl.BlockSpec((None, tb, bw), sec(0)),
                  pl.BlockSpec((None, tb, bw), sec(1)),
                  pl.BlockSpec((None, tb, bw), sec(2)),
                  pl.BlockSpec((None, tb, bw), sec(3)),
                  pl.BlockSpec((depth1, bw), lambda hg, t: (0, hg)),
                  pl.BlockSpec((1, bw), lambda hg, t: (0, hg)),
                  pl.BlockSpec(m2.shape, lambda hg, t: (0, 0))],
        out_specs=pl.BlockSpec((tb, bw), lambda hg, t: (t, hg)),
        out_shape=jax.ShapeDtypeStruct((s, width), BF16),
        scratch_shapes=[pltpu.VMEM((HG_PAIR, HG_DV, HG_DK), F32)],
        compiler_params=pltpu.CompilerParams(
            dimension_semantics=("arbitrary", "arbitrary"),
            vmem_limit_bytes=_vmem_limit(blk + 8 * 1024 * 1024)),
        name="hgrn2",
    )(proj, proj, proj, proj, lb_logits, norm_g.reshape(1, width), m2)


def kernel(x, attn_norm_g, w_in, conv_w, conv_b, conv_ln_g, conv_ln_b, hg_lb_logits, hg_norm_g,
           w_out, ffn_norm_g, w_gate, w_up, w_down, final_norm_g):
    bsz, seq, d_model = x.shape
    assert bsz == 1
    depth = w_in.shape[0]
    conv_width = conv_w.shape[2]
    hg_width = hg_norm_g.shape[1]
    d_ff = w_gate.shape[2]
    bf = 256
    assert d_ff % bf == 0

    xs = x.reshape(seq, d_model)
    for l in range(depth):
        h = _rmsnorm(xs, attn_norm_g[l], BF16)
        proj, w_out_b, w_down_b = _matmul([h], w_in[l], None, F32, bm=2048, bn=512, lhs_bufs=1,
                                          side_casts=(w_out[l], w_down[l]), side_rows=128,
                                          out_col_tile=HG_PAIR * HG_DK)
        a_out = _conformer_conv(proj, conv_w[l], conv_b[l], conv_ln_g[l], conv_ln_b[l], conv_width)
        b_out = _hgrn2(proj, hg_lb_logits, hg_norm_g[l], l, 2 * conv_width, hg_width)
        xs, h, ssq = _outproj_norm(a_out, b_out, w_out_b, xs, ffn_norm_g[l], bm=1024, bn=1024)
        u = _swiglu(h, ssq, w_gate[l], w_up[l], bm=4096, bf=bf)
        xs = _matmul([u], w_down_b, xs, F32, bm=512, bn=512)
    out = _rmsnorm(xs, final_norm_g, F32)
    return out.reshape(bsz, seq, d_model)
```

```python
import functools

import numpy as np
import jax
import jax.numpy as jnp
from jax import lax
from jax.experimental import pallas as pl
from jax.experimental.pallas import tpu as pltpu

F32 = jnp.float32
BF16 = jnp.bfloat16
EPS = 1e-6

LANES = 128
SUBLANES = 8
VMEM_BYTES = 64 * 1024 * 1024
VMEM_HEADROOM = 6 * 1024 * 1024

HG_DK = 128
HG_DV = 128
HG_PAIR = 2
CHUNK = 64
N_LEVELS = 6
CONV_K = 31
CONV_HALO = 32


def _vmem_limit(block_bytes):
    return int(min(VMEM_BYTES - 2 * 1024 * 1024, 2 * block_bytes + VMEM_HEADROOM))


def _sigmoid(x):
    return 1.0 / (1.0 + jnp.exp(-x))


def _silu(x):
    hx = 0.5 * x
    return hx + hx * jnp.tanh(hx)


def _rmsnorm_kernel(x_ref, g_ref, o_ref):
    x = x_ref[...]
    ms = jnp.mean(x * x, axis=-1, keepdims=True)
    o_ref[...] = (x * lax.rsqrt(ms + EPS) * g_ref[...]).astype(o_ref.dtype)


def _rmsnorm(x, g, out_dtype, bm=512):
    m, d = x.shape
    return pl.pallas_call(
        _rmsnorm_kernel,
        grid=(m // bm,),
        in_specs=[pl.BlockSpec((bm, d), lambda i: (i, 0)),
                  pl.BlockSpec((1, d), lambda i: (0, 0))],
        out_specs=pl.BlockSpec((bm, d), lambda i: (i, 0)),
        out_shape=jax.ShapeDtypeStruct((m, d), out_dtype),
        compiler_params=pltpu.CompilerParams(
            dimension_semantics=("arbitrary",),
            vmem_limit_bytes=_vmem_limit(bm * d * 10)),
        name="rmsnorm",
    )(x, g.reshape(1, d))


MATMUL_ROW_GROUPS = 4


def _matmul_kernel(*refs, has_res, side_blocks):
    a_ref, w_ref = refs[:2]
    res_ref = refs[2] if has_res else None
    n_in = 2 + has_res
    n_side = len(side_blocks)
    o_ref = refs[n_in + n_side]
    step = pl.program_id(0) * pl.num_programs(1) + pl.program_id(1)
    for s_in, s_out, n_blk in zip(refs[n_in:n_in + n_side], refs[n_in + n_side + 1:], side_blocks):
        @pl.when(step < n_blk)
        def _(s_in=s_in, s_out=s_out):
            s_out[...] = s_in[...].astype(s_out.dtype)
    w_tile = w_ref[...].astype(BF16)
    rows = a_ref.shape[0] // MATMUL_ROW_GROUPS
    for r in range(MATMUL_ROW_GROUPS):
        rs = slice(r * rows, (r + 1) * rows)
        acc = jnp.dot(a_ref[rs, :], w_tile, preferred_element_type=F32)
        if has_res:
            acc = acc + res_ref[rs, :]
        if len(o_ref.shape) == 3:
            ct = o_ref.shape[2]
            for t in range(o_ref.shape[0]):
                o_ref[t, rs, :] = acc[:, t * ct:(t + 1) * ct].astype(o_ref.dtype)
        else:
            o_ref[rs, :] = acc.astype(o_ref.dtype)


def _matmul(a, w, res, out_dtype, bm, bn, lhs_bufs=2, side_casts=(), side_rows=256, out_col_tile=None):
    m, kdim = a.shape
    n = w.shape[1]
    assert m % bm == 0 and n % bn == 0 and w.shape[0] == kdim and bm % MATMUL_ROW_GROUPS == 0
    w_bytes = jnp.dtype(w.dtype).itemsize
    in_specs = [pl.BlockSpec((bm, kdim), lambda i, j: (i, 0), pipeline_mode=pl.Buffered(lhs_bufs)),
                pl.BlockSpec((kdim, bn), lambda i, j: (0, j))]
    args = [a, w]
    w_tmp = 1 if w.dtype != BF16 else 0
    blk = bm * kdim * lhs_bufs + kdim * bn * (w_bytes + w_tmp) + bm * bn * jnp.dtype(out_dtype).itemsize
    if res is not None:
        in_specs.append(pl.BlockSpec((bm, bn), lambda i, j: (i, j)))
        args.append(res)
        blk += bm * bn * 4
    nj = n // bn
    if out_col_tile is None:
        out_specs = [pl.BlockSpec((bm, bn), lambda i, j: (i, j))]
        out_shape = [jax.ShapeDtypeStruct((m, n), out_dtype)]
    else:
        assert bn % out_col_tile == 0
        out_specs = [pl.BlockSpec((bn // out_col_tile, bm, out_col_tile), lambda i, j: (j, i, 0))]
        out_shape = [jax.ShapeDtypeStruct((n // out_col_tile, m, out_col_tile), out_dtype)]
    for s in side_casts:
        rows, cols = s.shape
        n_blk = rows // side_rows
        assert rows % side_rows == 0 and n_blk <= (m // bm) * nj
        side_map = lambda i, j, n_blk=n_blk: (jnp.minimum(i * nj + j, n_blk - 1), 0)
        in_specs.append(pl.BlockSpec((side_rows, cols), side_map))
        out_specs.append(pl.BlockSpec((side_rows, cols), side_map))
        out_shape.append(jax.ShapeDtypeStruct((rows, cols), BF16))
        args.append(s)
        blk += side_rows * cols * 6
    outs = pl.pallas_call(
        functools.partial(_matmul_kernel, has_res=res is not None,
                          side_blocks=tuple(s.shape[0] // side_rows for s in side_casts)),
        grid=(m // bm, nj),
        in_specs=in_specs,
        out_specs=out_specs,
        out_shape=out_shape,
        compiler_params=pltpu.CompilerParams(
            dimension_semantics=("arbitrary", "arbitrary"),
            vmem_limit_bytes=_vmem_limit(blk + bm * bn * 4)),
        name="matmul",
    )(*args)
    return outs if side_casts else outs[0]


def _outproj_kernel(a_ref, b_ref, wa_ref, wb_ref, res_ref, g_ref, x1_ref, hp_ref, ssq_ref):
    j = pl.program_id(1)

    @pl.when(j == 0)
    def _():
        ssq_ref[...] = jnp.zeros_like(ssq_ref)

    wa = wa_ref[...]
    wb = wb_ref[...]
    g = g_ref[...]
    rows = a_ref.shape[0] // MATMUL_ROW_GROUPS
    for r in range(MATMUL_ROW_GROUPS):
        rs = slice(r * rows, (r + 1) * rows)
        x1 = (jnp.dot(a_ref[rs, :], wa, preferred_element_type=F32)
              + jnp.dot(b_ref[rs, :], wb, preferred_element_type=F32)) + res_ref[rs, :]
        x1_ref[rs, :] = x1
        hp_ref[rs, :] = (x1 * g).astype(hp_ref.dtype)
        sq = x1 * x1
        part = sq[:, :LANES]
        for t in range(1, sq.shape[1] // LANES):
            part = part + sq[:, t * LANES:(t + 1) * LANES]
        ssq_ref[rs, :] += part


def _outproj_norm(a, b, w, res, g, bm, bn):
    m, ka = a.shape
    n = w.shape[1]
    assert b.shape == (m, ka) and w.shape[0] == 2 * ka and m % bm == 0 and n % bn == 0
    blk = 2 * bm * ka * 2 + 2 * ka * bn * 2 + bm * bn * (4 + 4 + 2) + bm * LANES * 4
    return pl.pallas_call(
        _outproj_kernel,
        grid=(m // bm, n // bn),
        in_specs=[pl.BlockSpec((bm, ka), lambda i, j: (i, 0)),
                  pl.BlockSpec((bm, ka), lambda i, j: (i, 0)),
                  pl.BlockSpec((ka, bn), lambda i, j: (0, j)),
                  pl.BlockSpec((ka, bn), lambda i, j: (1, j)),
                  pl.BlockSpec((bm, bn), lambda i, j: (i, j)),
                  pl.BlockSpec((1, bn), lambda i, j: (0, j))],
        out_specs=[pl.BlockSpec((bm, bn), lambda i, j: (i, j)),
                   pl.BlockSpec((bm, bn), lambda i, j: (i, j)),
                   pl.BlockSpec((bm, LANES), lambda i, j: (i, 0))],
        out_shape=[jax.ShapeDtypeStruct((m, n), F32),
                   jax.ShapeDtypeStruct((m, n), BF16),
                   jax.ShapeDtypeStruct((m, LANES), F32)],
        compiler_params=pltpu.CompilerParams(
            dimension_semantics=("arbitrary", "arbitrary"),
            vmem_limit_bytes=_vmem_limit(blk + bm * bn * 4)),
        name="outproj_norm",
    )(a, b, w, w, res, g.reshape(1, n))


SWIGLU_ROW_SPLITS = 4


def _swiglu_kernel(h_ref, ssq_ref, wg_ref, wu_ref, u_ref):
    wg = wg_ref[...].astype(BF16)
    wu = wu_ref[...].astype(BF16)
    rows = h_ref.shape[0] // SWIGLU_ROW_SPLITS
    d = h_ref.shape[1]
    for r in range(SWIGLU_ROW_SPLITS):
        rs = slice(r * rows, (r + 1) * rows)
        h = h_ref[rs, :]
        inv = lax.rsqrt(jnp.sum(ssq_ref[rs, :], axis=-1, keepdims=True) / d + EPS)
        g = jnp.dot(h, wg, preferred_element_type=F32) * inv
        u = jnp.dot(h, wu, preferred_element_type=F32) * inv
        u_ref[rs, :] = (_silu(g) * u).astype(u_ref.dtype)


def _swiglu(h, ssq, w_gate, w_up, bm, bf):
    m, d = h.shape
    d_ff = w_gate.shape[1]
    assert m % bm == 0 and d_ff % bf == 0 and ssq.shape == (m, LANES)
    w_bytes = jnp.dtype(w_gate.dtype).itemsize
    blk = bm * d + 2 * d * bf * (w_bytes + 1) + bm * bf * 2 + bm * LANES * 4
    return pl.pallas_call(
        _swiglu_kernel,
        grid=(m // bm, d_ff // bf),
        in_specs=[pl.BlockSpec((bm, d), lambda i, j: (i, 0), pipeline_mode=pl.Buffered(1)),
                  pl.BlockSpec((bm, LANES), lambda i, j: (i, 0)),
                  pl.BlockSpec((d, bf), lambda i, j: (0, j)),
                  pl.BlockSpec((d, bf), lambda i, j: (0, j))],
        out_specs=pl.BlockSpec((bm, bf), lambda i, j: (i, j)),
        out_shape=jax.ShapeDtypeStruct((m, d_ff), BF16),
        compiler_params=pltpu.CompilerParams(
            dimension_semantics=("arbitrary", "arbitrary"),
            vmem_limit_bytes=_vmem_limit(blk + bm * 2 * bf * 4)),
        name="swiglu",
    )(h, ssq, w_gate, w_up)


CONV_CW = 128
CONV_R = 128


def _conv_kernel(xv_ref, xg_ref, hv_ref, hg_ref, w_ref, b_ref, lg_ref, lb_ref, o_ref,
                 hext_ref, sh_ref, y_ref, *, t_rows, width):
    i = pl.program_id(0)
    ct = xv_ref.shape[2]
    for t in range(xv_ref.shape[0]):
        ts = slice(t * ct, (t + 1) * ct)
        hext_ref[CONV_HALO:, ts] = xv_ref[t] * _sigmoid(xg_ref[t])
        halo = hv_ref[t] * _sigmoid(hg_ref[t])
        hext_ref[:CONV_HALO, ts] = jnp.where(i > 0, halo, 0.0)

    first = CONV_HALO - (CONV_K - 1)
    n_sh = t_rows + CONV_HALO - SUBLANES
    for c in range(width // CONV_CW):
        cs = slice(c * CONV_CW, (c + 1) * CONV_CW)
        for b in range(1, SUBLANES):
            sh_ref[b - 1] = hext_ref[pl.ds(b, n_sh), cs]

        def row_tile(r, carry, cs=cs):
            r0 = pl.multiple_of(r * CONV_R, CONV_R)
            acc = jnp.zeros((CONV_R, CONV_CW), F32)
            for k in range(CONV_K):
                a, b = divmod(first + k, SUBLANES)
                if b == 0:
                    src = hext_ref[pl.ds(r0 + SUBLANES * a, CONV_R), cs]
                else:
                    src = sh_ref[b - 1, pl.ds(r0 + SUBLANES * a, CONV_R), :]
                acc = acc + w_ref[k:k + 1, cs] * src
            y_ref[pl.ds(r0, CONV_R), cs] = acc + b_ref[:, cs]
            return carry

        lax.fori_loop(0, t_rows // CONV_R, row_tile, 0)

    y = y_ref[...]
    mu = jnp.mean(y, axis=-1, keepdims=True)
    yc = y - mu
    var = jnp.mean(yc * yc, axis=-1, keepdims=True)
    z = yc * lax.rsqrt(var + EPS) * lg_ref[...] + lb_ref[...]
    o_ref[...] = (z * _sigmoid(z)).astype(o_ref.dtype)


def _conformer_conv(proj, conv_w, conv_b, ln_g, ln_b, width, t_rows=256):
    _, s, ct = proj.shape
    nt = width // ct
    hb = t_rows // CONV_HALO
    row = lambda a: a.reshape(1, width)
    halo_map = lambda c: (lambda i: (c, jnp.maximum(i * hb - 1, 0), 0))
    blk = (2 * t_rows + 2 * CONV_HALO + 40) * width * 4 + t_rows * width * 2
    scratch = ((t_rows + CONV_HALO) * width + 7 * (t_rows + CONV_HALO) * CONV_CW + t_rows * width) * 4
    return pl.pallas_call(
        functools.partial(_conv_kernel, t_rows=t_rows, width=width),
        grid=(s // t_rows,),
        in_specs=[pl.BlockSpec((nt, t_rows, ct), lambda i: (0, i, 0)),
                  pl.BlockSpec((nt, t_rows, ct), lambda i: (1, i, 0)),
                  pl.BlockSpec((nt, CONV_HALO, ct), halo_map(0)),
                  pl.BlockSpec((nt, CONV_HALO, ct), halo_map(1)),
                  pl.BlockSpec((CONV_K, width), lambda i: (0, 0)),
                  pl.BlockSpec((1, width), lambda i: (0, 0)),
                  pl.BlockSpec((1, width), lambda i: (0, 0)),
                  pl.BlockSpec((1, width), lambda i: (0, 0))],
        out_specs=pl.BlockSpec((t_rows, width), lambda i: (i, 0)),
        out_shape=jax.ShapeDtypeStruct((s, width), BF16),
        scratch_shapes=[pltpu.VMEM((t_rows + CONV_HALO, width), F32),
                        pltpu.VMEM((SUBLANES - 1, t_rows + CONV_HALO - SUBLANES, CONV_CW), F32),
                        pltpu.VMEM((t_rows, width), F32)],
        compiler_params=pltpu.CompilerParams(
            dimension_semantics=("arbitrary",),
            vmem_limit_bytes=_vmem_limit(blk + scratch)),
        name="conformer_conv",
    )(proj, proj, proj, proj, conv_w, row(conv_b), row(ln_g), row(ln_b))


HG_MXU_LEVELS = (3, 4)


def _decay_sum_matrix():
    c = CHUNK
    m = np.zeros((1 + len(HG_MXU_LEVELS), c, c), np.float32)
    for t in range(c):
        m[0, t, :t + 1] = 1.0
        for i, l in enumerate(HG_MXU_LEVELS):
            blk = c >> l
            mid = t - t % blk + blk // 2
            if t >= mid:
                m[1 + i, t, mid:t + 1] = 1.0
            else:
                m[1 + i, t, t + 1:mid] = 1.0
    m = m.reshape(-1, c)
    return np.concatenate([m, m], axis=1)


def _hgrn_kernel(q_ref, f_ref, v_ref, g_ref, lbl_ref, ng_ref, m_ref, o_ref, st_ref, *, tb, layer):
    t_blk = pl.program_id(1)

    @pl.when(t_blk == 0)
    def _():
        st_ref[...] = jnp.zeros_like(st_ref)

    c = CHUNK
    n_chunks = tb // c
    w = HG_PAIR * HG_DK
    heads = [slice(h * HG_DK, (h + 1) * HG_DK) for h in range(HG_PAIR)]
    lbl = lbl_ref[...]
    e = jnp.exp(lbl - jnp.max(lbl, axis=0, keepdims=True))
    lb = jnp.sum(e[:layer + 1], axis=0, keepdims=True) / jnp.sum(e, axis=0, keepdims=True)
    f_half = 0.5 * (1.0 - lb)
    f_mid = lb + f_half
    ng = ng_ref[...]
    m2 = m_ref[...]

    row = lax.broadcasted_iota(jnp.int32, (c, w), 0)
    upper_row = {l: (row % (c >> l)) >= (c >> (l + 1)) for l in range(3, N_LEVELS)}
    t_i = lax.broadcasted_iota(jnp.int32, (c, HG_PAIR * c), 0)
    s_i = lax.broadcasted_iota(jnp.int32, (c, HG_PAIR * c), 1) % c
    valid = []
    for l in range(N_LEVELS):
        blk = c >> l
        valid.append((t_i // blk == s_i // blk) & (t_i % blk >= blk // 2) & (s_i % blk < blk // 2))

    def blockdiag(a, zero):
        return jnp.concatenate([jnp.concatenate([a[:, heads[0]], zero], axis=1),
                                jnp.concatenate([zero, a[:, heads[1]]], axis=1)], axis=0)

    def per_head(fn):
        return jnp.concatenate([fn(hs) for hs in heads], axis=1)

    zero_c = jnp.zeros((c, HG_DK), BF16)
    zero_s = jnp.zeros((HG_DV, HG_DK), BF16)
    nt_dims = (((1,), (1,)), ((), ()))
    tn_dims = (((0,), (0,)), ((), ()))

    def level_operand(l, qf, kk, f, b_cum, ex):
        blk = c >> l
        half = blk // 2
        if blk >= 2 * SUBLANES:
            pieces, expo = [], []
            for s0 in range(0, c, blk):
                ref = jnp.broadcast_to(b_cum[s0 + half - 1:s0 + half, :], (half, w))
                lo = slice(s0, s0 + half)
                hi = slice(s0 + half, s0 + blk)
                pieces += [kk[lo, :], qf[hi, :]]
                expo += [ref - b_cum[lo, :], b_cum[hi, :] - ref]
            y = jnp.concatenate(pieces, axis=0) * jnp.exp2(jnp.concatenate(expo, axis=0))
        elif l in HG_MXU_LEVELS:
            i = HG_MXU_LEVELS.index(l)
            y = jnp.where(upper_row[l], qf, kk) * jnp.exp2(ex[(1 + i) * c:(2 + i) * c])
        else:
            y = jnp.where(upper_row[l], qf * f, kk)
        return y.astype(BF16)

    st = [st_ref[h] for h in range(HG_PAIR)]
    n_pairs = n_chunks // 2
    fronts, raws, updates, mids, st_ins = {}, {}, {}, {}, {}

    def stage_gates(p):
        out = []
        for ci in (2 * p, 2 * p + 1):
            rs = slice(ci * c, (ci + 1) * c)
            qf = _silu(q_ref[rs, :])
            f = f_mid + f_half * jnp.tanh(0.5 * f_ref[rs, :])
            log_f = jnp.log2(f)
            p0 = log_f.astype(BF16)
            p1 = (log_f - p0.astype(F32)).astype(BF16)
            ex = jnp.dot(m2, jnp.concatenate([p0, p1], axis=0), preferred_element_type=F32)
            out.append((rs, qf, 1.0 - f, f, ex[:c], ex))
        fronts[p] = out

    def stage_levels(p):
        raws[p] = []
        for l in range(N_LEVELS):
            ys = [level_operand(l, qf, kk, f, b_cum, ex) for (_, qf, kk, f, b_cum, ex) in fronts[p]]
            raws[p].append(lax.dot_general(jnp.concatenate(ys, axis=0),
                                           jnp.concatenate([blockdiag(y, zero_c) for y in ys], axis=0),
                                           nt_dims, preferred_element_type=F32))
        updates[p] = []
        for rs, qf, kk, f, b_cum, ex in fronts[p]:
            ke = (kk * jnp.exp2(b_cum[c - 1:c, :] - b_cum)).astype(BF16)
            updates[p].append(lax.dot_general(v_ref[rs, :].astype(BF16), ke, tn_dims,
                                              preferred_element_type=F32))

    def stage_intra(p):
        mids[p] = []
        for j, (rs, qf, kk, f, b_cum, ex) in enumerate(fronts[p]):
            scores = 0.0
            for l in range(N_LEVELS):
                s_l = raws[p][l][j * c:(j + 1) * c, j * HG_PAIR * c:(j + 1) * HG_PAIR * c]
                scores = jnp.where(valid[l], s_l, scores)
            v = v_ref[rs, :]
            o = jnp.dot(scores.astype(BF16), blockdiag(v.astype(BF16), zero_c),
                        preferred_element_type=F32)
            qk = qf * kk
            o = o + per_head(lambda hs: jnp.broadcast_to(
                jnp.sum(qk[:, hs], axis=-1, keepdims=True), (c, HG_DK))) * v
            qe = (qf * jnp.exp2(b_cum)).astype(BF16)
            mids[p].append((rs, o, qe, jnp.exp2(b_cum[c - 1:c, :])))

    def stage_state(p):
        st_ins[p] = []
        for (rs, o, qe, dec), ut in zip(mids[p], updates[p]):
            s0, s1 = (jnp.transpose(s).astype(BF16) for s in st)
            st_ins[p].append(jnp.concatenate([jnp.concatenate([s0, zero_s], axis=1),
                                              jnp.concatenate([zero_s, s1], axis=1)], axis=0))
            for h, hs in enumerate(heads):
                st[h] = st[h] * dec[:, hs] + ut[hs, hs]

    def stage_out(p):
        for (rs, o, qe, dec), st_bd in zip(mids[p], st_ins[p]):
            o = o + jnp.dot(qe, st_bd, preferred_element_type=F32)
            o2 = o * o
            inv = per_head(lambda hs: jnp.broadcast_to(
                lax.rsqrt(jnp.mean(o2[:, hs], axis=-1, keepdims=True) + EPS), (c, HG_DV)))
            o_ref[rs, :] = (o * inv * ng * _silu(g_ref[rs, :])).astype(o_ref.dtype)

    stages = (stage_gates, stage_levels, stage_intra, stage_state, stage_out)
    for tick in range(n_pairs + len(stages) - 1):
        for k, stage in enumerate(stages):
            if 0 <= tick - k < n_pairs:
                stage(tick - k)
    for h in range(HG_PAIR):
        st_ref[h] = st[h]


def _hgrn2(proj, lb_logits, norm_g, layer, col0, width, tb=2048):
    _, s, bw = proj.shape
    n_heads = width // HG_DK
    assert bw == HG_PAIR * HG_DK
    assert n_heads % HG_PAIR == 0 and s % tb == 0 and tb % CHUNK == 0 and col0 % bw == 0
    sec = lambda k: (lambda hg, t: ((col0 + k * width) // bw + hg, t, 0))
    depth1 = lb_logits.shape[0]
    m2 = jnp.asarray(_decay_sum_matrix(), BF16)
    blk = 4 * tb * bw * 4 + tb * bw * 2 + m2.size * 2
    return pl.pallas_call(
        functools.partial(_hgrn_kernel, tb=tb, layer=layer),
        grid=(n_heads // HG_PAIR, s // tb),
        in_specs=[pl.BlockSpec((None, tb, bw), sec(0)),
                  pl.BlockSpec((None, tb, bw), sec(1)),
                  pl.BlockSpec((None, tb, bw), sec(2)),
                  pl.BlockSpec((None, tb, bw), sec(3)),
                  pl.BlockSpec((depth1, bw), lambda hg, t: (0, hg)),
                  pl.BlockSpec((1, bw), lambda hg, t: (0, hg)),
                  pl.BlockSpec(m2.shape, lambda hg, t: (0, 0))],
        out_specs=pl.BlockSpec((tb, bw), lambda hg, t: (t, hg)),
        out_shape=jax.ShapeDtypeStruct((s, width), BF16),
        scratch_shapes=[pltpu.VMEM((HG_PAIR, HG_DV, HG_DK), F32)],
        compiler_params=pltpu.CompilerParams(
            dimension_semantics=("arbitrary", "arbitrary"),
            vmem_limit_bytes=_vmem_limit(blk + 8 * 1024 * 1024)),
        name="hgrn2",
    )(proj, proj, proj, proj, lb_logits, norm_g.reshape(1, width), m2)


def kernel(x, attn_norm_g, w_in, conv_w, conv_b, conv_ln_g, conv_ln_b, hg_lb_logits, hg_norm_g,
           w_out, ffn_norm_g, w_gate, w_up, w_down, final_norm_g):
    bsz, seq, d_model = x.shape
    assert bsz == 1
    depth = w_in.shape[0]
    conv_width = conv_w.shape[2]
    hg_width = hg_norm_g.shape[1]
    d_ff = w_gate.shape[2]
    bf = 256
    assert d_ff % bf == 0

    xs = x.reshape(seq, d_model)
    for l in range(depth):
        h = _rmsnorm(xs, attn_norm_g[l], BF16)
        proj, w_out_b, w_down_b = _matmul(h, w_in[l], None, F32, bm=2048, bn=512, lhs_bufs=1,
                                          side_casts=(w_out[l], w_down[l]), side_rows=128,
                                          out_col_tile=HG_PAIR * HG_DK)
        a_out = _conformer_conv(proj, conv_w[l], conv_b[l], conv_ln_g[l], conv_ln_b[l], conv_width)
        b_out = _hgrn2(proj, hg_lb_logits, hg_norm_g[l], l, 2 * conv_width, hg_width)
        xs, h, ssq = _outproj_norm(a_out, b_out, w_out_b, xs, ffn_norm_g[l], bm=1024, bn=1024)
        u = _swiglu(h, ssq, w_gate[l], w_up[l], bm=4096, bf=bf)
        xs = _matmul(u, w_down_b, xs, F32, bm=512, bn=512)
    out = _rmsnorm(xs, final_norm_g, F32)
    return out.reshape(bsz, seq, d_model)
```

```python
import functools

import numpy as np
import jax
import jax.numpy as jnp
from jax import lax
from jax.experimental import pallas as pl
from jax.experimental.pallas import tpu as pltpu

F32 = jnp.float32
BF16 = jnp.bfloat16
EPS = 1e-6

LANES = 128
SUBLANES = 8
VMEM_BYTES = 64 * 1024 * 1024
VMEM_HEADROOM = 6 * 1024 * 1024

HG_DK = 128
HG_DV = 128
HG_PAIR = 2
CHUNK = 64
N_LEVELS = CHUNK.bit_length() - 1
CONV_K = 31
CONV_HALO = 32


def _vmem_limit(block_bytes):
    return int(min(VMEM_BYTES - 2 * 1024 * 1024, 2 * block_bytes + VMEM_HEADROOM))


def _sigmoid(x):
    return 1.0 / (1.0 + jnp.exp(-x))


def _silu(x):
    hx = 0.5 * x
    return hx + hx * jnp.tanh(hx)


def _rmsnorm_kernel(x_ref, g_ref, o_ref):
    x = x_ref[...]
    ms = jnp.mean(x * x, axis=-1, keepdims=True)
    o_ref[...] = (x * lax.rsqrt(ms + EPS) * g_ref[...]).astype(o_ref.dtype)


def _rmsnorm(x, g, out_dtype, bm=512):
    m, d = x.shape
    return pl.pallas_call(
        _rmsnorm_kernel,
        grid=(m // bm,),
        in_specs=[pl.BlockSpec((bm, d), lambda i: (i, 0)),
                  pl.BlockSpec((1, d), lambda i: (0, 0))],
        out_specs=pl.BlockSpec((bm, d), lambda i: (i, 0)),
        out_shape=jax.ShapeDtypeStruct((m, d), out_dtype),
        compiler_params=pltpu.CompilerParams(
            dimension_semantics=("arbitrary",),
            vmem_limit_bytes=_vmem_limit(bm * d * 10)),
        name="rmsnorm",
    )(x, g.reshape(1, d))


OUTPROJ_ROW_GROUPS = 4


def _matmul_kernel(*refs, has_res, side_blocks, row_groups):
    a_ref, w_ref = refs[:2]
    res_ref = refs[2] if has_res else None
    n_in = 2 + has_res
    n_side = len(side_blocks)
    o_ref = refs[n_in + n_side]
    step = pl.program_id(0) * pl.num_programs(1) + pl.program_id(1)
    for s_in, s_out, n_blk in zip(refs[n_in:n_in + n_side], refs[n_in + n_side + 1:], side_blocks):
        @pl.when(step < n_blk)
        def _(s_in=s_in, s_out=s_out):
            s_out[...] = s_in[...].astype(s_out.dtype)
    w_tile = w_ref[...].astype(BF16)
    rows = a_ref.shape[0] // row_groups
    for r in range(row_groups):
        rs = slice(r * rows, (r + 1) * rows)
        acc = jnp.dot(a_ref[rs, :], w_tile, preferred_element_type=F32)
        if has_res:
            acc = acc + res_ref[rs, :]
        if len(o_ref.shape) == 3:
            ct = o_ref.shape[2]
            for t in range(o_ref.shape[0]):
                o_ref[t, rs, :] = acc[:, t * ct:(t + 1) * ct].astype(o_ref.dtype)
        else:
            o_ref[rs, :] = acc.astype(o_ref.dtype)


def _matmul(a, w, res, out_dtype, bm, bn, lhs_bufs=2, side_casts=(), side_rows=256, out_col_tile=None,
            row_groups=1):
    m, kdim = a.shape
    n = w.shape[1]
    assert m % bm == 0 and n % bn == 0 and w.shape[0] == kdim and bm % row_groups == 0
    w_bytes = jnp.dtype(w.dtype).itemsize
    in_specs = [pl.BlockSpec((bm, kdim), lambda i, j: (i, 0), pipeline_mode=pl.Buffered(lhs_bufs)),
                pl.BlockSpec((kdim, bn), lambda i, j: (0, j))]
    args = [a, w]
    w_tmp = 1 if w.dtype != BF16 else 0
    blk = bm * kdim * lhs_bufs + kdim * bn * (w_bytes + w_tmp) + bm * bn * jnp.dtype(out_dtype).itemsize
    if res is not None:
        in_specs.append(pl.BlockSpec((bm, bn), lambda i, j: (i, j)))
        args.append(res)
        blk += bm * bn * 4
    nj = n // bn
    if out_col_tile is None:
        out_specs = [pl.BlockSpec((bm, bn), lambda i, j: (i, j))]
        out_shape = [jax.ShapeDtypeStruct((m, n), out_dtype)]
    else:
        assert bn % out_col_tile == 0
        out_specs = [pl.BlockSpec((bn // out_col_tile, bm, out_col_tile), lambda i, j: (j, i, 0))]
        out_shape = [jax.ShapeDtypeStruct((n // out_col_tile, m, out_col_tile), out_dtype)]
    for s in side_casts:
        rows, cols = s.shape
        n_blk = rows // side_rows
        assert rows % side_rows == 0 and n_blk <= (m // bm) * nj
        side_map = lambda i, j, n_blk=n_blk: (jnp.minimum(i * nj + j, n_blk - 1), 0)
        in_specs.append(pl.BlockSpec((side_rows, cols), side_map))
        out_specs.append(pl.BlockSpec((side_rows, cols), side_map))
        out_shape.append(jax.ShapeDtypeStruct((rows, cols), BF16))
        args.append(s)
        blk += side_rows * cols * 6
    outs = pl.pallas_call(
        functools.partial(_matmul_kernel, has_res=res is not None, row_groups=row_groups,
                          side_blocks=tuple(s.shape[0] // side_rows for s in side_casts)),
        grid=(m // bm, nj),
        in_specs=in_specs,
        out_specs=out_specs,
        out_shape=out_shape,
        compiler_params=pltpu.CompilerParams(
            dimension_semantics=("arbitrary", "arbitrary"),
            vmem_limit_bytes=_vmem_limit(blk + bm * bn * 4)),
        name="matmul",
    )(*args)
    return outs if side_casts else outs[0]


def _outproj_kernel(a_ref, b_ref, wa_ref, wb_ref, res_ref, g_ref, x1_ref, hp_ref, ssq_ref):
    j = pl.program_id(1)

    @pl.when(j == 0)
    def _():
        ssq_ref[...] = jnp.zeros_like(ssq_ref)

    wa = wa_ref[...]
    wb = wb_ref[...]
    g = g_ref[...]
    rows = a_ref.shape[0] // OUTPROJ_ROW_GROUPS
    for r in range(OUTPROJ_ROW_GROUPS):
        rs = slice(r * rows, (r + 1) * rows)
        x1 = (jnp.dot(a_ref[rs, :], wa, preferred_element_type=F32)
              + jnp.dot(b_ref[rs, :], wb, preferred_element_type=F32)) + res_ref[rs, :]
        x1_ref[rs, :] = x1
        hp_ref[rs, :] = (x1 * g).astype(hp_ref.dtype)
        sq = x1 * x1
        part = sq[:, :LANES]
        for t in range(1, sq.shape[1] // LANES):
            part = part + sq[:, t * LANES:(t + 1) * LANES]
        ssq_ref[rs, :] += part


def _outproj_norm(a, b, w, res, g, bm, bn):
    m, ka = a.shape
    n = w.shape[1]
    assert b.shape == (m, ka) and w.shape[0] == 2 * ka and m % bm == 0 and n % bn == 0
    blk = 2 * bm * ka * 2 + 2 * ka * bn * 2 + bm * bn * (4 + 4 + 2) + bm * LANES * 4
    return pl.pallas_call(
        _outproj_kernel,
        grid=(m // bm, n // bn),
        in_specs=[pl.BlockSpec((bm, ka), lambda i, j: (i, 0)),
                  pl.BlockSpec((bm, ka), lambda i, j: (i, 0)),
                  pl.BlockSpec((ka, bn), lambda i, j: (0, j)),
                  pl.BlockSpec((ka, bn), lambda i, j: (1, j)),
                  pl.BlockSpec((bm, bn), lambda i, j: (i, j)),
                  pl.BlockSpec((1, bn), lambda i, j: (0, j))],
        out_specs=[pl.BlockSpec((bm, bn), lambda i, j: (i, j)),
                   pl.BlockSpec((bm, bn), lambda i, j: (i, j)),
                   pl.BlockSpec((bm, LANES), lambda i, j: (i, 0))],
        out_shape=[jax.ShapeDtypeStruct((m, n), F32),
                   jax.ShapeDtypeStruct((m, n), BF16),
                   jax.ShapeDtypeStruct((m, LANES), F32)],
        compiler_params=pltpu.CompilerParams(
            dimension_semantics=("arbitrary", "arbitrary"),
            vmem_limit_bytes=_vmem_limit(blk + bm * bn * 4)),
        name="outproj_norm",
    )(a, b, w, w, res, g.reshape(1, n))


SWIGLU_ROW_SPLITS = 4


def _swiglu_kernel(h_ref, ssq_ref, wg_ref, wu_ref, u_ref):
    wg = wg_ref[...].astype(BF16)
    wu = wu_ref[...].astype(BF16)
    rows = h_ref.shape[0] // SWIGLU_ROW_SPLITS
    d = h_ref.shape[1]
    for r in range(SWIGLU_ROW_SPLITS):
        rs = slice(r * rows, (r + 1) * rows)
        h = h_ref[rs, :]
        inv = lax.rsqrt(jnp.sum(ssq_ref[rs, :], axis=-1, keepdims=True) / d + EPS)
        g = jnp.dot(h, wg, preferred_element_type=F32) * inv
        u = jnp.dot(h, wu, preferred_element_type=F32) * inv
        u_ref[rs, :] = (_silu(g) * u).astype(u_ref.dtype)


def _swiglu(h, ssq, w_gate, w_up, bm, bf):
    m, d = h.shape
    d_ff = w_gate.shape[1]
    assert m % bm == 0 and d_ff % bf == 0 and ssq.shape == (m, LANES)
    w_bytes = jnp.dtype(w_gate.dtype).itemsize
    blk = bm * d + 2 * d * bf * (w_bytes + 1) + bm * bf * 2 + bm * LANES * 4
    return pl.pallas_call(
        _swiglu_kernel,
        grid=(m // bm, d_ff // bf),
        in_specs=[pl.BlockSpec((bm, d), lambda i, j: (i, 0), pipeline_mode=pl.Buffered(1)),
                  pl.BlockSpec((bm, LANES), lambda i, j: (i, 0)),
                  pl.BlockSpec((d, bf), lambda i, j: (0, j)),
                  pl.BlockSpec((d, bf), lambda i, j: (0, j))],
        out_specs=pl.BlockSpec((bm, bf), lambda i, j: (i, j)),
        out_shape=jax.ShapeDtypeStruct((m, d_ff), BF16),
        compiler_params=pltpu.CompilerParams(
            dimension_semantics=("arbitrary", "arbitrary"),
            vmem_limit_bytes=_vmem_limit(blk + bm * 2 * bf * 4)),
        name="swiglu",
    )(h, ssq, w_gate, w_up)


CONV_CW = 128
CONV_R = 128


def _conv_kernel(xv_ref, xg_ref, hv_ref, hg_ref, w_ref, b_ref, lg_ref, lb_ref, o_ref,
                 hext_ref, sh_ref, y_ref, *, t_rows, width):
    i = pl.program_id(0)
    ct = xv_ref.shape[2]
    for t in range(xv_ref.shape[0]):
        ts = slice(t * ct, (t + 1) * ct)
        hext_ref[CONV_HALO:, ts] = xv_ref[t] * _sigmoid(xg_ref[t])
        halo = hv_ref[t] * _sigmoid(hg_ref[t])
        hext_ref[:CONV_HALO, ts] = jnp.where(i > 0, halo, 0.0)

    first = CONV_HALO - (CONV_K - 1)
    n_sh = t_rows + CONV_HALO - SUBLANES
    for c in range(width // CONV_CW):
        cs = slice(c * CONV_CW, (c + 1) * CONV_CW)
        for b in range(1, SUBLANES):
            sh_ref[b - 1] = hext_ref[pl.ds(b, n_sh), cs]

        def row_tile(r, carry, cs=cs):
            r0 = pl.multiple_of(r * CONV_R, CONV_R)
            acc = jnp.zeros((CONV_R, CONV_CW), F32)
            for k in range(CONV_K):
                a, b = divmod(first + k, SUBLANES)
                if b == 0:
                    src = hext_ref[pl.ds(r0 + SUBLANES * a, CONV_R), cs]
                else:
                    src = sh_ref[b - 1, pl.ds(r0 + SUBLANES * a, CONV_R), :]
                acc = acc + w_ref[k:k + 1, cs] * src
            y_ref[pl.ds(r0, CONV_R), cs] = acc + b_ref[:, cs]
            return carry

        lax.fori_loop(0, t_rows // CONV_R, row_tile, 0)

    y = y_ref[...]
    mu = jnp.mean(y, axis=-1, keepdims=True)
    yc = y - mu
    var = jnp.mean(yc * yc, axis=-1, keepdims=True)
    z = yc * lax.rsqrt(var + EPS) * lg_ref[...] + lb_ref[...]
    o_ref[...] = _silu(z).astype(o_ref.dtype)


def _conformer_conv(proj, conv_w, conv_b, ln_g, ln_b, width, t_rows=256):
    _, s, ct = proj.shape
    nt = width // ct
    hb = t_rows // CONV_HALO
    row = lambda a: a.reshape(1, width)
    halo_map = lambda c: (lambda i: (c, jnp.maximum(i * hb - 1, 0), 0))
    blk = (2 * t_rows + 2 * CONV_HALO + 40) * width * 4 + t_rows * width * 2
    scratch = ((t_rows + CONV_HALO) * width + 7 * (t_rows + CONV_HALO) * CONV_CW + t_rows * width) * 4
    return pl.pallas_call(
        functools.partial(_conv_kernel, t_rows=t_rows, width=width),
        grid=(s // t_rows,),
        in_specs=[pl.BlockSpec((nt, t_rows, ct), lambda i: (0, i, 0)),
                  pl.BlockSpec((nt, t_rows, ct), lambda i: (1, i, 0)),
                  pl.BlockSpec((nt, CONV_HALO, ct), halo_map(0)),
                  pl.BlockSpec((nt, CONV_HALO, ct), halo_map(1)),
                  pl.BlockSpec((CONV_K, width), lambda i: (0, 0)),
                  pl.BlockSpec((1, width), lambda i: (0, 0)),
                  pl.BlockSpec((1, width), lambda i: (0, 0)),
                  pl.BlockSpec((1, width), lambda i: (0, 0))],
        out_specs=pl.BlockSpec((t_rows, width), lambda i: (i, 0)),
        out_shape=jax.ShapeDtypeStruct((s, width), BF16),
        scratch_shapes=[pltpu.VMEM((t_rows + CONV_HALO, width), F32),
                        pltpu.VMEM((SUBLANES - 1, t_rows + CONV_HALO - SUBLANES, CONV_CW), F32),
                        pltpu.VMEM((t_rows, width), F32)],
        compiler_params=pltpu.CompilerParams(
            dimension_semantics=("arbitrary",),
            vmem_limit_bytes=_vmem_limit(blk + scratch)),
        name="conformer_conv",
    )(proj, proj, proj, proj, conv_w, row(conv_b), row(ln_g), row(ln_b))


HG_MXU_LEVELS = (3, 4)


def _decay_sum_matrix():
    c = CHUNK
    m = np.zeros((1 + len(HG_MXU_LEVELS), c, c), np.float32)
    for t in range(c):
        m[0, t, :t + 1] = 1.0
        for i, l in enumerate(HG_MXU_LEVELS):
            blk = c >> l
            mid = t - t % blk + blk // 2
            if t >= mid:
                m[1 + i, t, mid:t + 1] = 1.0
            else:
                m[1 + i, t, t + 1:mid] = 1.0
    m = m.reshape(-1, c)
    return np.concatenate([m, m], axis=1)


def _hgrn_kernel(q_ref, f_ref, v_ref, g_ref, lbl_ref, ng_ref, m_ref, o_ref, st_ref, *, tb, layer):
    t_blk = pl.program_id(1)

    @pl.when(t_blk == 0)
    def _():
        st_ref[...] = jnp.zeros_like(st_ref)

    c = CHUNK
    n_chunks = tb // c
    w = HG_PAIR * HG_DK
    heads = [slice(h * HG_DK, (h + 1) * HG_DK) for h in range(HG_PAIR)]
    lbl = lbl_ref[...]
    e = jnp.exp(lbl - jnp.max(lbl, axis=0, keepdims=True))
    lb = jnp.sum(e[:layer + 1], axis=0, keepdims=True) / jnp.sum(e, axis=0, keepdims=True)
    f_half = 0.5 * (1.0 - lb)
    f_mid = lb + f_half
    ng = ng_ref[...]
    m2 = m_ref[...]

    row = lax.broadcasted_iota(jnp.int32, (c, w), 0)
    upper_row = {l: (row % (c >> l)) >= (c >> (l + 1)) for l in range(3, N_LEVELS)}
    t_i = lax.broadcasted_iota(jnp.int32, (c, HG_PAIR * c), 0)
    s_i = lax.broadcasted_iota(jnp.int32, (c, HG_PAIR * c), 1) % c
    valid = []
    for l in range(N_LEVELS):
        blk = c >> l
        valid.append((t_i // blk == s_i // blk) & (t_i % blk >= blk // 2) & (s_i % blk < blk // 2))

    def blockdiag(a, zero):
        return jnp.concatenate([jnp.concatenate([a[:, heads[0]], zero], axis=1),
                                jnp.concatenate([zero, a[:, heads[1]]], axis=1)], axis=0)

    def per_head(fn):
        return jnp.concatenate([fn(hs) for hs in heads], axis=1)

    zero_c = jnp.zeros((c, HG_DK), BF16)
    zero_s = jnp.zeros((HG_DV, HG_DK), BF16)
    nt_dims = (((1,), (1,)), ((), ()))
    tn_dims = (((0,), (0,)), ((), ()))

    def level_operand(l, qf, kk, f, b_cum, ex):
        blk = c >> l
        half = blk // 2
        if blk >= 2 * SUBLANES:
            pieces, expo = [], []
            for s0 in range(0, c, blk):
                ref = jnp.broadcast_to(b_cum[s0 + half - 1:s0 + half, :], (half, w))
                lo = slice(s0, s0 + half)
                hi = slice(s0 + half, s0 + blk)
                pieces += [kk[lo, :], qf[hi, :]]
                expo += [ref - b_cum[lo, :], b_cum[hi, :] - ref]
            y = jnp.concatenate(pieces, axis=0) * jnp.exp2(jnp.concatenate(expo, axis=0))
        elif l in HG_MXU_LEVELS:
            i = HG_MXU_LEVELS.index(l)
            y = jnp.where(upper_row[l], qf, kk) * jnp.exp2(ex[(1 + i) * c:(2 + i) * c])
        else:
            y = jnp.where(upper_row[l], qf * f, kk)
        return y.astype(BF16)

    st = [st_ref[h] for h in range(HG_PAIR)]
    n_pairs = n_chunks // 2
    fronts, raws, updates, mids, st_ins = {}, {}, {}, {}, {}

    def stage_gates(p):
        out = []
        for ci in (2 * p, 2 * p + 1):
            rs = slice(ci * c, (ci + 1) * c)
            qf = _silu(q_ref[rs, :])
            f = f_mid + f_half * jnp.tanh(0.5 * f_ref[rs, :])
            log_f = jnp.log2(f)
            p0 = log_f.astype(BF16)
            p1 = (log_f - p0.astype(F32)).astype(BF16)
            ex = jnp.dot(m2, jnp.concatenate([p0, p1], axis=0), preferred_element_type=F32)
            out.append((rs, qf, 1.0 - f, f, ex[:c], ex))
        fronts[p] = out

    def stage_levels(p):
        raws[p] = []
        for l in range(N_LEVELS):
            ys = [level_operand(l, qf, kk, f, b_cum, ex) for (_, qf, kk, f, b_cum, ex) in fronts[p]]
            raws[p].append(lax.dot_general(jnp.concatenate(ys, axis=0),
                                           jnp.concatenate([blockdiag(y, zero_c) for y in ys], axis=0),
                                           nt_dims, preferred_element_type=F32))
        updates[p] = []
        for rs, qf, kk, f, b_cum, ex in fronts[p]:
            ke = (kk * jnp.exp2(b_cum[c - 1:c, :] - b_cum)).astype(BF16)
            updates[p].append(lax.dot_general(v_ref[rs, :].astype(BF16), ke, tn_dims,
                                              preferred_element_type=F32))

    def stage_intra(p):
        mids[p] = []
        for j, (rs, qf, kk, f, b_cum, ex) in enumerate(fronts[p]):
            scores = 0.0
            for l in range(N_LEVELS):
                s_l = raws[p][l][j * c:(j + 1) * c, j * HG_PAIR * c:(j + 1) * HG_PAIR * c]
                scores = jnp.where(valid[l], s_l, scores)
            v = v_ref[rs, :]
            o = jnp.dot(scores.astype(BF16), blockdiag(v.astype(BF16), zero_c),
                        preferred_element_type=F32)
            qk = qf * kk
            o = o + per_head(lambda hs: jnp.broadcast_to(
                jnp.sum(qk[:, hs], axis=-1, keepdims=True), (c, HG_DK))) * v
            qe = (qf * jnp.exp2(b_cum)).astype(BF16)
            mids[p].append((rs, o, qe, jnp.exp2(b_cum[c - 1:c, :])))

    def stage_state(p):
        st_ins[p] = []
        for (rs, o, qe, dec), ut in zip(mids[p], updates[p]):
            s0, s1 = (jnp.transpose(s).astype(BF16) for s in st)
            st_ins[p].append(jnp.concatenate([jnp.concatenate([s0, zero_s], axis=1),
                                              jnp.concatenate([zero_s, s1], axis=1)], axis=0))
            for h, hs in enumerate(heads):
                st[h] = st[h] * dec[:, hs] + ut[hs, hs]

    def stage_out(p):
        for (rs, o, qe, dec), st_bd in zip(mids[p], st_ins[p]):
            o = o + jnp.dot(qe, st_bd, preferred_element_type=F32)
            o2 = o * o
            inv = per_head(lambda hs: jnp.broadcast_to(
                lax.rsqrt(jnp.mean(o2[:, hs], axis=-1, keepdims=True) + EPS), (c, HG_DV)))
            o_ref[rs, :] = (o * inv * ng * _silu(g_ref[rs, :])).astype(o_ref.dtype)

    stages = (stage_gates, stage_levels, stage_intra, stage_state, stage_out)
    for tick in range(n_pairs + len(stages) - 1):
        for k, stage in enumerate(stages):
            if 0 <= tick - k < n_pairs:
                stage(tick - k)
    for h in range(HG_PAIR):
        st_ref[h] = st[h]


def _hgrn2(proj, lb_logits, norm_g, layer, col0, width, tb=2048):
    _, s, bw = proj.shape
    n_heads = width // HG_DK
    assert bw == HG_PAIR * HG_DK
    assert n_heads % HG_PAIR == 0 and s % tb == 0 and tb % CHUNK == 0 and col0 % bw == 0
    sec = lambda k: (lambda hg, t: ((col0 + k * width) // bw + hg, t, 0))
    depth1 = lb_logits.shape[0]
    m2 = jnp.asarray(_decay_sum_matrix(), BF16)
    blk = 4 * tb * bw * 4 + tb * bw * 2 + m2.size * 2
    return pl.pallas_call(
        functools.partial(_hgrn_kernel, tb=tb, layer=layer),
        grid=(n_heads // HG_PAIR, s // tb),
        in_specs=[pl.BlockSpec((None, tb, bw), sec(0)),
                  pl.BlockSpec((None, tb, bw), sec(1)),
                  pl.BlockSpec((None, tb, bw), sec(2)),
                  pl.BlockSpec((None, tb, bw), sec(3)),
                  pl.BlockSpec((depth1, bw), lambda hg, t: (0, hg)),
                  pl.BlockSpec((1, bw), lambda hg, t: (0, hg)),
                  pl.BlockSpec(m2.shape, lambda hg, t: (0, 0))],
        out_specs=pl.BlockSpec((tb, bw), lambda hg, t: (t, hg)),
        out_shape=jax.ShapeDtypeStruct((s, width), BF16),
        scratch_shapes=[pltpu.VMEM((HG_PAIR, HG_DV, HG_DK), F32)],
        compiler_params=pltpu.CompilerParams(
            dimension_semantics=("arbitrary", "arbitrary"),
            vmem_limit_bytes=_vmem_limit(blk + 8 * 1024 * 1024)),
        name="hgrn2",
    )(proj, proj, proj, proj, lb_logits, norm_g.reshape(1, width), m2)


def kernel(x, attn_norm_g, w_in, conv_w, conv_b, conv_ln_g, conv_ln_b, hg_lb_logits, hg_norm_g,
           w_out, ffn_norm_g, w_gate, w_up, w_down, final_norm_g):
    bsz, seq, d_model = x.shape
    assert bsz == 1
    depth = w_in.shape[0]
    conv_width = conv_w.shape[2]
    hg_width = hg_norm_g.shape[1]
    d_ff = w_gate.shape[2]
    bf = 256
    assert d_ff % bf == 0

    xs = x.reshape(seq, d_model)
    for l in range(depth):
        h = _rmsnorm(xs, attn_norm_g[l], BF16)
        proj, w_out_b, w_down_b = _matmul(h, w_in[l], None, F32, bm=2048, bn=512, lhs_bufs=1,
                                          side_casts=(w_out[l], w_down[l]), side_rows=128,
                                          out_col_tile=HG_PAIR * HG_DK, row_groups=4)
        a_out = _conformer_conv(proj, conv_w[l], conv_b[l], conv_ln_g[l], conv_ln_b[l], conv_width)
        b_out = _hgrn2(proj, hg_lb_logits, hg_norm_g[l], l, 2 * conv_width, hg_width)
        xs, h, ssq = _outproj_norm(a_out, b_out, w_out_b, xs, ffn_norm_g[l], bm=1024, bn=1024)
        u = _swiglu(h, ssq, w_gate[l], w_up[l], bm=4096, bf=bf)
        xs = _matmul(u, w_down_b, xs, F32, bm=512, bn=512)
    out = _rmsnorm(xs, final_norm_g, F32)
    return out.reshape(bsz, seq, d_model)
```

```python
import functools

import numpy as np
import jax
import jax.numpy as jnp
from jax import lax
from jax.experimental import pallas as pl
from jax.experimental.pallas import tpu as pltpu

F32 = jnp.float32
BF16 = jnp.bfloat16
EPS = 1e-6

LANES = 128
SUBLANES = 8
VMEM_BYTES = 64 * 1024 * 1024
VMEM_HEADROOM = 6 * 1024 * 1024

HG_DK = 128
HG_DV = 128
HG_PAIR = 2
CHUNK = 64
N_LEVELS = CHUNK.bit_length() - 1
CONV_K = 31
CONV_HALO = 32


def _vmem_limit(block_bytes):
    return int(min(VMEM_BYTES - 2 * 1024 * 1024, 2 * block_bytes + VMEM_HEADROOM))


def _sigmoid(x):
    return 1.0 / (1.0 + jnp.exp(-x))


def _silu(x):
    hx = 0.5 * x
    return hx + hx * jnp.tanh(hx)


def _rmsnorm_kernel(x_ref, g_ref, o_ref):
    x = x_ref[...]
    ms = jnp.mean(x * x, axis=-1, keepdims=True)
    o_ref[...] = (x * lax.rsqrt(ms + EPS) * g_ref[...]).astype(o_ref.dtype)


def _rmsnorm(x, g, out_dtype, bm=512):
    m, d = x.shape
    return pl.pallas_call(
        _rmsnorm_kernel,
        grid=(m // bm,),
        in_specs=[pl.BlockSpec((bm, d), lambda i: (i, 0)),
                  pl.BlockSpec((1, d), lambda i: (0, 0))],
        out_specs=pl.BlockSpec((bm, d), lambda i: (i, 0)),
        out_shape=jax.ShapeDtypeStruct((m, d), out_dtype),
        compiler_params=pltpu.CompilerParams(
            dimension_semantics=("arbitrary",),
            vmem_limit_bytes=_vmem_limit(bm * d * 10)),
        name="rmsnorm",
    )(x, g.reshape(1, d))


OUTPROJ_ROW_GROUPS = 4


def _matmul_kernel(*refs, has_res, side_blocks, row_groups):
    a_ref, w_ref = refs[:2]
    res_ref = refs[2] if has_res else None
    n_in = 2 + has_res
    n_side = len(side_blocks)
    o_ref = refs[n_in + n_side]
    step = pl.program_id(0) * pl.num_programs(1) + pl.program_id(1)
    for s_in, s_out, n_blk in zip(refs[n_in:n_in + n_side], refs[n_in + n_side + 1:], side_blocks):
        @pl.when(step < n_blk)
        def _(s_in=s_in, s_out=s_out):
            s_out[...] = s_in[...].astype(s_out.dtype)
    w_tile = w_ref[...].astype(BF16)
    rows = a_ref.shape[0] // row_groups
    for r in range(row_groups):
        rs = slice(r * rows, (r + 1) * rows)
        acc = jnp.dot(a_ref[rs, :], w_tile, preferred_element_type=F32)
        if has_res:
            acc = acc + res_ref[rs, :]
        if len(o_ref.shape) == 3:
            ct = o_ref.shape[2]
            for t in range(o_ref.shape[0]):
                o_ref[t, rs, :] = acc[:, t * ct:(t + 1) * ct].astype(o_ref.dtype)
        else:
            o_ref[rs, :] = acc.astype(o_ref.dtype)


def _matmul(a, w, res, out_dtype, bm, bn, lhs_bufs=2, side_casts=(), side_rows=256, out_col_tile=None,
            row_groups=1):
    m, kdim = a.shape
    n = w.shape[1]
    assert m % bm == 0 and n % bn == 0 and w.shape[0] == kdim and bm % row_groups == 0
    w_bytes = jnp.dtype(w.dtype).itemsize
    in_specs = [pl.BlockSpec((bm, kdim), lambda i, j: (i, 0), pipeline_mode=pl.Buffered(lhs_bufs)),
                pl.BlockSpec((kdim, bn), lambda i, j: (0, j))]
    args = [a, w]
    w_tmp = 1 if w.dtype != BF16 else 0
    blk = bm * kdim * lhs_bufs + kdim * bn * (w_bytes + w_tmp) + bm * bn * jnp.dtype(out_dtype).itemsize
    if res is not None:
        in_specs.append(pl.BlockSpec((bm, bn), lambda i, j: (i, j)))
        args.append(res)
        blk += bm * bn * 4
    nj = n // bn
    if out_col_tile is None:
        out_specs = [pl.BlockSpec((bm, bn), lambda i, j: (i, j))]
        out_shape = [jax.ShapeDtypeStruct((m, n), out_dtype)]
    else:
        assert bn % out_col_tile == 0
        out_specs = [pl.BlockSpec((bn // out_col_tile, bm, out_col_tile), lambda i, j: (j, i, 0))]
        out_shape = [jax.ShapeDtypeStruct((n // out_col_tile, m, out_col_tile), out_dtype)]
    for s in side_casts:
        rows, cols = s.shape
        n_blk = rows // side_rows
        assert rows % side_rows == 0 and n_blk <= (m // bm) * nj
        side_map = lambda i, j, n_blk=n_blk: (jnp.minimum(i * nj + j, n_blk - 1), 0)
        in_specs.append(pl.BlockSpec((side_rows, cols), side_map))
        out_specs.append(pl.BlockSpec((side_rows, cols), side_map))
        out_shape.append(jax.ShapeDtypeStruct((rows, cols), BF16))
        args.append(s)
        blk += side_rows * cols * 6
    outs = pl.pallas_call(
        functools.partial(_matmul_kernel, has_res=res is not None, row_groups=row_groups,
                          side_blocks=tuple(s.shape[0] // side_rows for s in side_casts)),
        grid=(m // bm, nj),
        in_specs=in_specs,
        out_specs=out_specs,
        out_shape=out_shape,
        compiler_params=pltpu.CompilerParams(
            dimension_semantics=("arbitrary", "arbitrary"),
            vmem_limit_bytes=_vmem_limit(blk + bm * bn * 4)),
        name="matmul",
    )(*args)
    return outs if side_casts else outs[0]


def _outproj_kernel(a_ref, b_ref, wa_ref, wb_ref, res_ref, g_ref, x1_ref, hp_ref, ssq_ref):
    j = pl.program_id(1)

    @pl.when(j == 0)
    def _():
        ssq_ref[...] = jnp.zeros_like(ssq_ref)

    wa = wa_ref[...]
    wb = wb_ref[...]
    g = g_ref[...]
    rows = a_ref.shape[0] // OUTPROJ_ROW_GROUPS
    for r in range(OUTPROJ_ROW_GROUPS):
        rs = slice(r * rows, (r + 1) * rows)
        x1 = (jnp.dot(a_ref[rs, :], wa, preferred_element_type=F32)
              + jnp.dot(b_ref[rs, :], wb, preferred_element_type=F32)) + res_ref[rs, :]
        x1_ref[rs, :] = x1
        hp_ref[rs, :] = (x1 * g).astype(hp_ref.dtype)
        sq = x1 * x1
        part = sq[:, :LANES]
        for t in range(1, sq.shape[1] // LANES):
            part = part + sq[:, t * LANES:(t + 1) * LANES]
        ssq_ref[rs, :] += part


def _outproj_norm(a, b, w, res, g, bm, bn):
    m, ka = a.shape
    n = w.shape[1]
    assert b.shape == (m, ka) and w.shape[0] == 2 * ka and m % bm == 0 and n % bn == 0
    blk = 2 * bm * ka * 2 + 2 * ka * bn * 2 + bm * bn * (4 + 4 + 2) + bm * LANES * 4
    return pl.pallas_call(
        _outproj_kernel,
        grid=(m // bm, n // bn),
        in_specs=[pl.BlockSpec((bm, ka), lambda i, j: (i, 0)),
                  pl.BlockSpec((bm, ka), lambda i, j: (i, 0)),
                  pl.BlockSpec((ka, bn), lambda i, j: (0, j)),
                  pl.BlockSpec((ka, bn), lambda i, j: (1, j)),
                  pl.BlockSpec((bm, bn), lambda i, j: (i, j)),
                  pl.BlockSpec((1, bn), lambda i, j: (0, j))],
        out_specs=[pl.BlockSpec((bm, bn), lambda i, j: (i, j)),
                   pl.BlockSpec((bm, bn), lambda i, j: (i, j)),
                   pl.BlockSpec((bm, LANES), lambda i, j: (i, 0))],
        out_shape=[jax.ShapeDtypeStruct((m, n), F32),
                   jax.ShapeDtypeStruct((m, n), BF16),
                   jax.ShapeDtypeStruct((m, LANES), F32)],
        compiler_params=pltpu.CompilerParams(
            dimension_semantics=("arbitrary", "arbitrary"),
            vmem_limit_bytes=_vmem_limit(blk + bm * bn * 4)),
        name="outproj_norm",
    )(a, b, w, w, res, g.reshape(1, n))


SWIGLU_ROW_SPLITS = 4


def _swiglu_kernel(h_ref, ssq_ref, wg_ref, wu_ref, u_ref):
    wg = wg_ref[...].astype(BF16)
    wu = wu_ref[...].astype(BF16)
    rows = h_ref.shape[0] // SWIGLU_ROW_SPLITS
    d = h_ref.shape[1]
    for r in range(SWIGLU_ROW_SPLITS):
        rs = slice(r * rows, (r + 1) * rows)
        h = h_ref[rs, :]
        inv = lax.rsqrt(jnp.sum(ssq_ref[rs, :], axis=-1, keepdims=True) / d + EPS)
        g = jnp.dot(h, wg, preferred_element_type=F32) * inv
        u = jnp.dot(h, wu, preferred_element_type=F32) * inv
        u_ref[rs, :] = (_silu(g) * u).astype(u_ref.dtype)


def _swiglu(h, ssq, w_gate, w_up, bm, bf):
    m, d = h.shape
    d_ff = w_gate.shape[1]
    assert m % bm == 0 and d_ff % bf == 0 and ssq.shape == (m, LANES)
    w_bytes = jnp.dtype(w_gate.dtype).itemsize
    blk = bm * d + 2 * d * bf * (w_bytes + 1) + bm * bf * 2 + bm * LANES * 4
    return pl.pallas_call(
        _swiglu_kernel,
        grid=(m // bm, d_ff // bf),
        in_specs=[pl.BlockSpec((bm, d), lambda i, j: (i, 0), pipeline_mode=pl.Buffered(1)),
                  pl.BlockSpec((bm, LANES), lambda i, j: (i, 0)),
                  pl.BlockSpec((d, bf), lambda i, j: (0, j)),
                  pl.BlockSpec((d, bf), lambda i, j: (0, j))],
        out_specs=pl.BlockSpec((bm, bf), lambda i, j: (i, j)),
        out_shape=jax.ShapeDtypeStruct((m, d_ff), BF16),
        compiler_params=pltpu.CompilerParams(
            dimension_semantics=("arbitrary", "arbitrary"),
            vmem_limit_bytes=_vmem_limit(blk + bm * 2 * bf * 4)),
        name="swiglu",
    )(h, ssq, w_gate, w_up)


CONV_CW = 128
CONV_R = 128


def _conv_kernel(xv_ref, xg_ref, hv_ref, hg_ref, w_ref, b_ref, lg_ref, lb_ref, o_ref,
                 hext_ref, sh_ref, y_ref, *, t_rows, width):
    i = pl.program_id(0)
    ct = xv_ref.shape[2]
    for t in range(xv_ref.shape[0]):
        ts = slice(t * ct, (t + 1) * ct)
        hext_ref[CONV_HALO:, ts] = xv_ref[t] * _sigmoid(xg_ref[t])
        halo = hv_ref[t] * _sigmoid(hg_ref[t])
        hext_ref[:CONV_HALO, ts] = jnp.where(i > 0, halo, 0.0)

    first = CONV_HALO - (CONV_K - 1)
    n_sh = t_rows + CONV_HALO - SUBLANES
    for c in range(width // CONV_CW):
        cs = slice(c * CONV_CW, (c + 1) * CONV_CW)
        for b in range(1, SUBLANES):
            sh_ref[b - 1] = hext_ref[pl.ds(b, n_sh), cs]

        def row_tile(r, carry, cs=cs):
            r0 = pl.multiple_of(r * CONV_R, CONV_R)
            acc = jnp.zeros((CONV_R, CONV_CW), F32)
            for k in range(CONV_K):
                a, b = divmod(first + k, SUBLANES)
                if b == 0:
                    src = hext_ref[pl.ds(r0 + SUBLANES * a, CONV_R), cs]
                else:
                    src = sh_ref[b - 1, pl.ds(r0 + SUBLANES * a, CONV_R), :]
                acc = acc + w_ref[k:k + 1, cs] * src
            y_ref[pl.ds(r0, CONV_R), cs] = acc + b_ref[:, cs]
            return carry

        lax.fori_loop(0, t_rows // CONV_R, row_tile, 0)

    y = y_ref[...]
    mu = jnp.mean(y, axis=-1, keepdims=True)
    yc = y - mu
    var = jnp.mean(yc * yc, axis=-1, keepdims=True)
    z = yc * lax.rsqrt(var + EPS) * lg_ref[...] + lb_ref[...]
    o_ref[...] = _silu(z).astype(o_ref.dtype)


def _conformer_conv(proj, conv_w, conv_b, ln_g, ln_b, width, t_rows=256):
    _, s, ct = proj.shape
    nt = width // ct
    hb = t_rows // CONV_HALO
    row = lambda a: a.reshape(1, width)
    halo_map = lambda c: (lambda i: (c, jnp.maximum(i * hb - 1, 0), 0))
    blk = (2 * t_rows + 2 * CONV_HALO + 40) * width * 4 + t_rows * width * 2
    scratch = ((t_rows + CONV_HALO) * width + 7 * (t_rows + CONV_HALO) * CONV_CW + t_rows * width) * 4
    return pl.pallas_call(
        functools.partial(_conv_kernel, t_rows=t_rows, width=width),
        grid=(s // t_rows,),
        in_specs=[pl.BlockSpec((nt, t_rows, ct), lambda i: (0, i, 0)),
                  pl.BlockSpec((nt, t_rows, ct), lambda i: (1, i, 0)),
                  pl.BlockSpec((nt, CONV_HALO, ct), halo_map(0)),
                  pl.BlockSpec((nt, CONV_HALO, ct), halo_map(1)),
                  pl.BlockSpec((CONV_K, width), lambda i: (0, 0)),
                  pl.BlockSpec((1, width), lambda i: (0, 0)),
                  pl.BlockSpec((1, width), lambda i: (0, 0)),
                  pl.BlockSpec((1, width), lambda i: (0, 0))],
        out_specs=pl.BlockSpec((t_rows, width), lambda i: (i, 0)),
        out_shape=jax.ShapeDtypeStruct((s, width), BF16),
        scratch_shapes=[pltpu.VMEM((t_rows + CONV_HALO, width), F32),
                        pltpu.VMEM((SUBLANES - 1, t_rows + CONV_HALO - SUBLANES, CONV_CW), F32),
                        pltpu.VMEM((t_rows, width), F32)],
        compiler_params=pltpu.CompilerParams(
            dimension_semantics=("arbitrary",),
            vmem_limit_bytes=_vmem_limit(blk + scratch)),
        name="conformer_conv",
    )(proj, proj, proj, proj, conv_w, row(conv_b), row(ln_g), row(ln_b))


HG_MXU_LEVELS = (3, 4)


def _decay_sum_matrix():
    c = CHUNK
    m = np.zeros((1 + len(HG_MXU_LEVELS), c, c), np.float32)
    for t in range(c):
        m[0, t, :t + 1] = 1.0
        for i, l in enumerate(HG_MXU_LEVELS):
            blk = c >> l
            mid = t - t % blk + blk // 2
            if t >= mid:
                m[1 + i, t, mid:t + 1] = 1.0
            else:
                m[1 + i, t, t + 1:mid] = 1.0
    m = m.reshape(-1, c)
    return np.concatenate([m, m], axis=1)


def _hgrn_kernel(q_ref, f_ref, v_ref, g_ref, lbl_ref, ng_ref, m_ref, o_ref, st_ref, *, tb, layer):
    t_blk = pl.program_id(1)

    @pl.when(t_blk == 0)
    def _():
        st_ref[...] = jnp.zeros_like(st_ref)

    c = CHUNK
    n_chunks = tb // c
    w = HG_PAIR * HG_DK
    heads = [slice(h * HG_DK, (h + 1) * HG_DK) for h in range(HG_PAIR)]
    lbl = lbl_ref[...]
    e = jnp.exp(lbl - jnp.max(lbl, axis=0, keepdims=True))
    lb = jnp.sum(e[:layer + 1], axis=0, keepdims=True) / jnp.sum(e, axis=0, keepdims=True)
    f_half = 0.5 * (1.0 - lb)
    f_mid = lb + f_half
    ng = ng_ref[...]
    m2 = m_ref[...]

    row = lax.broadcasted_iota(jnp.int32, (c, w), 0)
    upper_row = {l: (row % (c >> l)) >= (c >> (l + 1)) for l in range(3, N_LEVELS)}
    t_i = lax.broadcasted_iota(jnp.int32, (c, HG_PAIR * c), 0)
    s_i = lax.broadcasted_iota(jnp.int32, (c, HG_PAIR * c), 1) % c
    valid = []
    for l in range(N_LEVELS):
        blk = c >> l
        valid.append((t_i // blk == s_i // blk) & (t_i % blk >= blk // 2) & (s_i % blk < blk // 2))

    def blockdiag(a, zero):
        return jnp.concatenate([jnp.concatenate([a[:, heads[0]], zero], axis=1),
                                jnp.concatenate([zero, a[:, heads[1]]], axis=1)], axis=0)

    def per_head(fn):
        return jnp.concatenate([fn(hs) for hs in heads], axis=1)

    zero_c = jnp.zeros((c, HG_DK), BF16)
    zero_s = jnp.zeros((HG_DV, HG_DK), BF16)
    nt_dims = (((1,), (1,)), ((), ()))
    tn_dims = (((0,), (0,)), ((), ()))

    def level_operand(l, qf, kk, f, b_cum, ex):
        blk = c >> l
        half = blk // 2
        if blk >= 2 * SUBLANES:
            pieces, expo = [], []
            for s0 in range(0, c, blk):
                ref = jnp.broadcast_to(b_cum[s0 + half - 1:s0 + half, :], (half, w))
                lo = slice(s0, s0 + half)
                hi = slice(s0 + half, s0 + blk)
                pieces += [kk[lo, :], qf[hi, :]]
                expo += [ref - b_cum[lo, :], b_cum[hi, :] - ref]
            y = jnp.concatenate(pieces, axis=0) * jnp.exp2(jnp.concatenate(expo, axis=0))
        elif l in HG_MXU_LEVELS:
            i = HG_MXU_LEVELS.index(l)
            y = jnp.where(upper_row[l], qf, kk) * jnp.exp2(ex[(1 + i) * c:(2 + i) * c])
        else:
            y = jnp.where(upper_row[l], qf * f, kk)
        return y.astype(BF16)

    st = [st_ref[h] for h in range(HG_PAIR)]
    n_pairs = n_chunks // 2
    fronts, raws, updates, mids, st_ins = {}, {}, {}, {}, {}

    def stage_gates(p):
        out = []
        for ci in (2 * p, 2 * p + 1):
            rs = slice(ci * c, (ci + 1) * c)
            qf = _silu(q_ref[rs, :])
            f = jnp.maximum(f_mid + f_half * jnp.tanh(0.5 * f_ref[rs, :]), lb)
            log_f = jnp.log2(f)
            p0 = log_f.astype(BF16)
            p1 = (log_f - p0.astype(F32)).astype(BF16)
            ex = jnp.dot(m2, jnp.concatenate([p0, p1], axis=0), preferred_element_type=F32)
            out.append((rs, qf, 1.0 - f, f, ex[:c], ex))
        fronts[p] = out

    def stage_levels(p):
        raws[p] = []
        for l in range(N_LEVELS):
            ys = [level_operand(l, qf, kk, f, b_cum, ex) for (_, qf, kk, f, b_cum, ex) in fronts[p]]
            raws[p].append(lax.dot_general(jnp.concatenate(ys, axis=0),
                                           jnp.concatenate([blockdiag(y, zero_c) for y in ys], axis=0),
                                           nt_dims, preferred_element_type=F32))
        updates[p] = []
        for rs, qf, kk, f, b_cum, ex in fronts[p]:
            ke = (kk * jnp.exp2(b_cum[c - 1:c, :] - b_cum)).astype(BF16)
            updates[p].append(lax.dot_general(v_ref[rs, :].astype(BF16), ke, tn_dims,
                                              preferred_element_type=F32))

    def stage_intra(p):
        mids[p] = []
        for j, (rs, qf, kk, f, b_cum, ex) in enumerate(fronts[p]):
            scores = 0.0
            for l in range(N_LEVELS):
                s_l = raws[p][l][j * c:(j + 1) * c, j * HG_PAIR * c:(j + 1) * HG_PAIR * c]
                scores = jnp.where(valid[l], s_l, scores)
            v = v_ref[rs, :]
            o = jnp.dot(scores.astype(BF16), blockdiag(v.astype(BF16), zero_c),
                        preferred_element_type=F32)
            qk = qf * kk
            o = o + per_head(lambda hs: jnp.broadcast_to(
                jnp.sum(qk[:, hs], axis=-1, keepdims=True), (c, HG_DK))) * v
            qe = (qf * jnp.exp2(b_cum)).astype(BF16)
            mids[p].append((rs, o, qe, jnp.exp2(b_cum[c - 1:c, :])))

    def stage_state(p):
        st_ins[p] = []
        for (rs, o, qe, dec), ut in zip(mids[p], updates[p]):
            s0, s1 = (jnp.transpose(s).astype(BF16) for s in st)
            st_ins[p].append(jnp.concatenate([jnp.concatenate([s0, zero_s], axis=1),
                                              jnp.concatenate([zero_s, s1], axis=1)], axis=0))
            for h, hs in enumerate(heads):
                st[h] = st[h] * dec[:, hs] + ut[hs, hs]

    def stage_out(p):
        for (rs, o, qe, dec), st_bd in zip(mids[p], st_ins[p]):
            o = o + jnp.dot(qe, st_bd, preferred_element_type=F32)
            o2 = o * o
            inv = per_head(lambda hs: jnp.broadcast_to(
                lax.rsqrt(jnp.mean(o2[:, hs], axis=-1, keepdims=True) + EPS), (c, HG_DV)))
            o_ref[rs, :] = (o * inv * ng * _silu(g_ref[rs, :])).astype(o_ref.dtype)

    stages = (stage_gates, stage_levels, stage_intra, stage_state, stage_out)
    for tick in range(n_pairs + len(stages) - 1):
        for k, stage in enumerate(stages):
            if 0 <= tick - k < n_pairs:
                stage(tick - k)
    for h in range(HG_PAIR):
        st_ref[h] = st[h]


def _hgrn2(proj, lb_logits, norm_g, layer, col0, width, tb=2048):
    _, s, bw = proj.shape
    n_heads = width // HG_DK
    assert bw == HG_PAIR * HG_DK
    assert n_heads % HG_PAIR == 0 and s % tb == 0 and tb % CHUNK == 0 and col0 % bw == 0
    sec = lambda k: (lambda hg, t: ((col0 + k * width) // bw + hg, t, 0))
    depth1 = lb_logits.shape[0]
    m2 = jnp.asarray(_decay_sum_matrix(), BF16)
    blk = 4 * tb * bw * 4 + tb * bw * 2 + m2.size * 2
    return pl.pallas_call(
        functools.partial(_hgrn_kernel, tb=tb, layer=layer),
        grid=(n_heads // HG_PAIR, s // tb),
        in_specs=[pl.BlockSpec((None, tb, bw), sec(0)),
                  pl.BlockSpec((None, tb, bw), sec(1)),
                  pl.BlockSpec((None, tb, bw), sec(2)),
                  pl.BlockSpec((None, tb, bw), sec(3)),
                  pl.BlockSpec((depth1, bw), lambda hg, t: (0, hg)),
                  pl.BlockSpec((1, bw), lambda hg, t: (0, hg)),
                  pl.BlockSpec(m2.shape, lambda hg, t: (0, 0))],
        out_specs=pl.BlockSpec((tb, bw), lambda hg, t: (t, hg)),
        out_shape=jax.ShapeDtypeStruct((s, width), BF16),
        scratch_shapes=[pltpu.VMEM((HG_PAIR, HG_DV, HG_DK), F32)],
        compiler_params=pltpu.CompilerParams(
            dimension_semantics=("arbitrary", "arbitrary"),
            vmem_limit_bytes=_vmem_limit(blk + 8 * 1024 * 1024)),
        name="hgrn2",
    )(proj, proj, proj, proj, lb_logits, norm_g.reshape(1, width), m2)


def kernel(x, attn_norm_g, w_in, conv_w, conv_b, conv_ln_g, conv_ln_b, hg_lb_logits, hg_norm_g,
           w_out, ffn_norm_g, w_gate, w_up, w_down, final_norm_g):
    bsz, seq, d_model = x.shape
    assert bsz == 1
    depth = w_in.shape[0]
    conv_width = conv_w.shape[2]
    hg_width = hg_norm_g.shape[1]
    d_ff = w_gate.shape[2]
    bf = 256
    assert d_ff % bf == 0

    xs = x.reshape(seq, d_model)
    for l in range(depth):
        h = _rmsnorm(xs, attn_norm_g[l], BF16)
        proj, w_out_b, w_down_b = _matmul(h, w_in[l], None, F32, bm=2048, bn=512, lhs_bufs=1,
                                          side_casts=(w_out[l], w_down[l]), side_rows=128,
                                          out_col_tile=HG_PAIR * HG_DK, row_groups=4)
        a_out = _conformer_conv(proj, conv_w[l], conv_b[l], conv_ln_g[l], conv_ln_b[l], conv_width)
        b_out = _hgrn2(proj, hg_lb_logits, hg_norm_g[l], l, 2 * conv_width, hg_width)
        xs, h, ssq = _outproj_norm(a_out, b_out, w_out_b, xs, ffn_norm_g[l], bm=1024, bn=1024)
        u = _swiglu(h, ssq, w_gate[l], w_up[l], bm=4096, bf=bf)
        xs = _matmul(u, w_down_b, xs, F32, bm=512, bn=512)
    out = _rmsnorm(xs, final_norm_g, F32)
    return out.reshape(bsz, seq, d_model)
```

```python
import functools

import numpy as np
import jax
import jax.numpy as jnp
from jax import lax
from jax.experimental import pallas as pl
from jax.experimental.pallas import tpu as pltpu

F32 = jnp.float32
BF16 = jnp.bfloat16
EPS = 1e-6

LANES = 128
SUBLANES = 8
VMEM_BYTES = 64 * 1024 * 1024
VMEM_HEADROOM = 6 * 1024 * 1024

HG_DK = 128
HG_DV = 128
HG_PAIR = 2
CHUNK = 64
N_LEVELS = CHUNK.bit_length() - 1
CONV_K = 31
CONV_HALO = 32


def _vmem_limit(block_bytes):
    return int(min(VMEM_BYTES - 2 * 1024 * 1024, 2 * block_bytes + VMEM_HEADROOM))


def _sigmoid(x):
    return 1.0 / (1.0 + jnp.exp(-x))


def _silu(x):
    hx = 0.5 * x
    return hx + hx * jnp.tanh(hx)


def _rmsnorm_kernel(x_ref, g_ref, o_ref):
    x = x_ref[...]
    ms = jnp.mean(x * x, axis=-1, keepdims=True)
    o_ref[...] = (x * lax.rsqrt(ms + EPS) * g_ref[...]).astype(o_ref.dtype)


def _rmsnorm(x, g, out_dtype, bm=512):
    m, d = x.shape
    return pl.pallas_call(
        _rmsnorm_kernel,
        grid=(m // bm,),
        in_specs=[pl.BlockSpec((bm, d), lambda i: (i, 0)),
                  pl.BlockSpec((1, d), lambda i: (0, 0))],
        out_specs=pl.BlockSpec((bm, d), lambda i: (i, 0)),
        out_shape=jax.ShapeDtypeStruct((m, d), out_dtype),
        compiler_params=pltpu.CompilerParams(
            dimension_semantics=("arbitrary",),
            vmem_limit_bytes=_vmem_limit(bm * d * 10)),
        name="rmsnorm",
    )(x, g.reshape(1, d))


OUTPROJ_ROW_GROUPS = 4


def _matmul_kernel(*refs, has_res, side_blocks, row_groups):
    a_ref, w_ref = refs[:2]
    res_ref = refs[2] if has_res else None
    n_in = 2 + has_res
    n_side = len(side_blocks)
    o_ref = refs[n_in + n_side]
    step = pl.program_id(0) * pl.num_programs(1) + pl.program_id(1)
    for s_in, s_out, n_blk in zip(refs[n_in:n_in + n_side], refs[n_in + n_side + 1:], side_blocks):
        @pl.when(step < n_blk)
        def _(s_in=s_in, s_out=s_out):
            s_out[...] = s_in[...].astype(s_out.dtype)
    w_tile = w_ref[...].astype(BF16)
    rows = a_ref.shape[0] // row_groups
    for r in range(row_groups):
        rs = slice(r * rows, (r + 1) * rows)
        acc = jnp.dot(a_ref[rs, :], w_tile, preferred_element_type=F32)
        if has_res:
            acc = acc + res_ref[rs, :]
        if len(o_ref.shape) == 3:
            ct = o_ref.shape[2]
            for t in range(o_ref.shape[0]):
                o_ref[t, rs, :] = acc[:, t * ct:(t + 1) * ct].astype(o_ref.dtype)
        else:
            o_ref[rs, :] = acc.astype(o_ref.dtype)


def _matmul(a, w, res, out_dtype, bm, bn, lhs_bufs=2, side_casts=(), side_rows=256, out_col_tile=None,
            row_groups=1):
    m, kdim = a.shape
    n = w.shape[1]
    assert m % bm == 0 and n % bn == 0 and w.shape[0] == kdim and bm % row_groups == 0
    w_bytes = jnp.dtype(w.dtype).itemsize
    in_specs = [pl.BlockSpec((bm, kdim), lambda i, j: (i, 0), pipeline_mode=pl.Buffered(lhs_bufs)),
                pl.BlockSpec((kdim, bn), lambda i, j: (0, j))]
    args = [a, w]
    w_tmp = 1 if w.dtype != BF16 else 0
    blk = bm * kdim * lhs_bufs + kdim * bn * (w_bytes + w_tmp) + bm * bn * jnp.dtype(out_dtype).itemsize
    if res is not None:
        in_specs.append(pl.BlockSpec((bm, bn), lambda i, j: (i, j)))
        args.append(res)
        blk += bm * bn * 4
    nj = n // bn
    if out_col_tile is None:
        out_specs = [pl.BlockSpec((bm, bn), lambda i, j: (i, j))]
        out_shape = [jax.ShapeDtypeStruct((m, n), out_dtype)]
    else:
        assert bn % out_col_tile == 0
        out_specs = [pl.BlockSpec((bn // out_col_tile, bm, out_col_tile), lambda i, j: (j, i, 0))]
        out_shape = [jax.ShapeDtypeStruct((n // out_col_tile, m, out_col_tile), out_dtype)]
    for s in side_casts:
        rows, cols = s.shape
        n_blk = rows // side_rows
        assert rows % side_rows == 0 and n_blk <= (m // bm) * nj
        side_map = lambda i, j, n_blk=n_blk: (jnp.minimum(i * nj + j, n_blk - 1), 0)
        in_specs.append(pl.BlockSpec((side_rows, cols), side_map))
        out_specs.append(pl.BlockSpec((side_rows, cols), side_map))
        out_shape.append(jax.ShapeDtypeStruct((rows, cols), BF16))
        args.append(s)
        blk += side_rows * cols * 6
    outs = pl.pallas_call(
        functools.partial(_matmul_kernel, has_res=res is not None, row_groups=row_groups,
                          side_blocks=tuple(s.shape[0] // side_rows for s in side_casts)),
        grid=(m // bm, nj),
        in_specs=in_specs,
        out_specs=out_specs,
        out_shape=out_shape,
        compiler_params=pltpu.CompilerParams(
            dimension_semantics=("parallel", "arbitrary"),
            vmem_limit_bytes=_vmem_limit(blk + bm * bn * 4)),
        name="matmul",
    )(*args)
    return outs if side_casts else outs[0]


def _outproj_kernel(a_ref, b_ref, wa_ref, wb_ref, res_ref, g_ref, x1_ref, hp_ref, ssq_ref):
    j = pl.program_id(1)

    @pl.when(j == 0)
    def _():
        ssq_ref[...] = jnp.zeros_like(ssq_ref)

    wa = wa_ref[...]
    wb = wb_ref[...]
    g = g_ref[...]
    rows = a_ref.shape[0] // OUTPROJ_ROW_GROUPS
    for r in range(OUTPROJ_ROW_GROUPS):
        rs = slice(r * rows, (r + 1) * rows)
        x1 = (jnp.dot(a_ref[rs, :], wa, preferred_element_type=F32)
              + jnp.dot(b_ref[rs, :], wb, preferred_element_type=F32)) + res_ref[rs, :]
        x1_ref[rs, :] = x1
        hp_ref[rs, :] = (x1 * g).astype(hp_ref.dtype)
        sq = x1 * x1
        part = sq[:, :LANES]
        for t in range(1, sq.shape[1] // LANES):
            part = part + sq[:, t * LANES:(t + 1) * LANES]
        ssq_ref[rs, :] += part


def _outproj_norm(a, b, w, res, g, bm, bn):
    m, ka = a.shape
    n = w.shape[1]
    assert b.shape == (m, ka) and w.shape[0] == 2 * ka and m % bm == 0 and n % bn == 0
    blk = 2 * bm * ka * 2 + 2 * ka * bn * 2 + bm * bn * (4 + 4 + 2) + bm * LANES * 4
    return pl.pallas_call(
        _outproj_kernel,
        grid=(m // bm, n // bn),
        in_specs=[pl.BlockSpec((bm, ka), lambda i, j: (i, 0)),
                  pl.BlockSpec((bm, ka), lambda i, j: (i, 0)),
                  pl.BlockSpec((ka, bn), lambda i, j: (0, j)),
                  pl.BlockSpec((ka, bn), lambda i, j: (1, j)),
                  pl.BlockSpec((bm, bn), lambda i, j: (i, j)),
                  pl.BlockSpec((1, bn), lambda i, j: (0, j))],
        out_specs=[pl.BlockSpec((bm, bn), lambda i, j: (i, j)),
                   pl.BlockSpec((bm, bn), lambda i, j: (i, j)),
                   pl.BlockSpec((bm, LANES), lambda i, j: (i, 0))],
        out_shape=[jax.ShapeDtypeStruct((m, n), F32),
                   jax.ShapeDtypeStruct((m, n), BF16),
                   jax.ShapeDtypeStruct((m, LANES), F32)],
        compiler_params=pltpu.CompilerParams(
            dimension_semantics=("parallel", "arbitrary"),
            vmem_limit_bytes=_vmem_limit(blk + bm * bn * 4)),
        name="outproj_norm",
    )(a, b, w, w, res, g.reshape(1, n))


SWIGLU_ROW_SPLITS = 4


def _swiglu_kernel(h_ref, ssq_ref, wg_ref, wu_ref, u_ref):
    wg = wg_ref[...].astype(BF16)
    wu = wu_ref[...].astype(BF16)
    rows = h_ref.shape[0] // SWIGLU_ROW_SPLITS
    d = h_ref.shape[1]
    for r in range(SWIGLU_ROW_SPLITS):
        rs = slice(r * rows, (r + 1) * rows)
        h = h_ref[rs, :]
        inv = lax.rsqrt(jnp.sum(ssq_ref[rs, :], axis=-1, keepdims=True) / d + EPS)
        g = jnp.dot(h, wg, preferred_element_type=F32) * inv
        u = jnp.dot(h, wu, preferred_element_type=F32) * inv
        u_ref[rs, :] = (_silu(g) * u).astype(u_ref.dtype)


def _swiglu(h, ssq, w_gate, w_up, bm, bf):
    m, d = h.shape
    d_ff = w_gate.shape[1]
    assert m % bm == 0 and d_ff % bf == 0 and ssq.shape == (m, LANES)
    w_bytes = jnp.dtype(w_gate.dtype).itemsize
    blk = bm * d + 2 * d * bf * (w_bytes + 1) + bm * bf * 2 + bm * LANES * 4
    return pl.pallas_call(
        _swiglu_kernel,
        grid=(m // bm, d_ff // bf),
        in_specs=[pl.BlockSpec((bm, d), lambda i, j: (i, 0), pipeline_mode=pl.Buffered(1)),
                  pl.BlockSpec((bm, LANES), lambda i, j: (i, 0)),
                  pl.BlockSpec((d, bf), lambda i, j: (0, j)),
                  pl.BlockSpec((d, bf), lambda i, j: (0, j))],
        out_specs=pl.BlockSpec((bm, bf), lambda i, j: (i, j)),
        out_shape=jax.ShapeDtypeStruct((m, d_ff), BF16),
        compiler_params=pltpu.CompilerParams(
            dimension_semantics=("parallel", "arbitrary"),
            vmem_limit_bytes=_vmem_limit(blk + bm * 2 * bf * 4)),
        name="swiglu",
    )(h, ssq, w_gate, w_up)


CONV_CW = 128
CONV_R = 128


def _conv_kernel(xv_ref, xg_ref, hv_ref, hg_ref, w_ref, b_ref, lg_ref, lb_ref, o_ref,
                 hext_ref, sh_ref, y_ref, *, t_rows, width):
    i = pl.program_id(0)
    ct = xv_ref.shape[2]
    for t in range(xv_ref.shape[0]):
        ts = slice(t * ct, (t + 1) * ct)
        hext_ref[CONV_HALO:, ts] = xv_ref[t] * _sigmoid(xg_ref[t])
        halo = hv_ref[t] * _sigmoid(hg_ref[t])
        hext_ref[:CONV_HALO, ts] = jnp.where(i > 0, halo, 0.0)

    first = CONV_HALO - (CONV_K - 1)
    n_sh = t_rows + CONV_HALO - SUBLANES
    for c in range(width // CONV_CW):
        cs = slice(c * CONV_CW, (c + 1) * CONV_CW)
        for b in range(1, SUBLANES):
            sh_ref[b - 1] = hext_ref[pl.ds(b, n_sh), cs]

        def row_tile(r, carry, cs=cs):
            r0 = pl.multiple_of(r * CONV_R, CONV_R)
            acc = jnp.zeros((CONV_R, CONV_CW), F32)
            for k in range(CONV_K):
                a, b = divmod(first + k, SUBLANES)
                if b == 0:
                    src = hext_ref[pl.ds(r0 + SUBLANES * a, CONV_R), cs]
                else:
                    src = sh_ref[b - 1, pl.ds(r0 + SUBLANES * a, CONV_R), :]
                acc = acc + w_ref[k:k + 1, cs] * src
            y_ref[pl.ds(r0, CONV_R), cs] = acc + b_ref[:, cs]
            return carry

        lax.fori_loop(0, t_rows // CONV_R, row_tile, 0)

    y = y_ref[...]
    mu = jnp.mean(y, axis=-1, keepdims=True)
    yc = y - mu
    var = jnp.mean(yc * yc, axis=-1, keepdims=True)
    z = yc * lax.rsqrt(var + EPS) * lg_ref[...] + lb_ref[...]
    o_ref[...] = _silu(z).astype(o_ref.dtype)


def _conformer_conv(proj, conv_w, conv_b, ln_g, ln_b, width, t_rows=256):
    _, s, ct = proj.shape
    nt = width // ct
    hb = t_rows // CONV_HALO
    row = lambda a: a.reshape(1, width)
    halo_map = lambda c: (lambda i: (c, jnp.maximum(i * hb - 1, 0), 0))
    blk = (2 * t_rows + 2 * CONV_HALO + 40) * width * 4 + t_rows * width * 2
    scratch = ((t_rows + CONV_HALO) * width + 7 * (t_rows + CONV_HALO) * CONV_CW + t_rows * width) * 4
    return pl.pallas_call(
        functools.partial(_conv_kernel, t_rows=t_rows, width=width),
        grid=(s // t_rows,),
        in_specs=[pl.BlockSpec((nt, t_rows, ct), lambda i: (0, i, 0)),
                  pl.BlockSpec((nt, t_rows, ct), lambda i: (1, i, 0)),
                  pl.BlockSpec((nt, CONV_HALO, ct), halo_map(0)),
                  pl.BlockSpec((nt, CONV_HALO, ct), halo_map(1)),
                  pl.BlockSpec((CONV_K, width), lambda i: (0, 0)),
                  pl.BlockSpec((1, width), lambda i: (0, 0)),
                  pl.BlockSpec((1, width), lambda i: (0, 0)),
                  pl.BlockSpec((1, width), lambda i: (0, 0))],
        out_specs=pl.BlockSpec((t_rows, width), lambda i: (i, 0)),
        out_shape=jax.ShapeDtypeStruct((s, width), BF16),
        scratch_shapes=[pltpu.VMEM((t_rows + CONV_HALO, width), F32),
                        pltpu.VMEM((SUBLANES - 1, t_rows + CONV_HALO - SUBLANES, CONV_CW), F32),
                        pltpu.VMEM((t_rows, width), F32)],
        compiler_params=pltpu.CompilerParams(
            dimension_semantics=("arbitrary",),
            vmem_limit_bytes=_vmem_limit(blk + scratch)),
        name="conformer_conv",
    )(proj, proj, proj, proj, conv_w, row(conv_b), row(ln_g), row(ln_b))


HG_MXU_LEVELS = (3, 4)


def _decay_sum_matrix():
    c = CHUNK
    m = np.zeros((1 + len(HG_MXU_LEVELS), c, c), np.float32)
    for t in range(c):
        m[0, t, :t + 1] = 1.0
        for i, l in enumerate(HG_MXU_LEVELS):
            blk = c >> l
            mid = t - t % blk + blk // 2
            if t >= mid:
                m[1 + i, t, mid:t + 1] = 1.0
            else:
                m[1 + i, t, t + 1:mid] = 1.0
    m = m.reshape(-1, c)
    return np.concatenate([m, m], axis=1)


def _hgrn_kernel(q_ref, f_ref, v_ref, g_ref, lbl_ref, ng_ref, m_ref, o_ref, st_ref, *, tb, layer):
    t_blk = pl.program_id(1)

    @pl.when(t_blk == 0)
    def _():
        st_ref[...] = jnp.zeros_like(st_ref)

    c = CHUNK
    n_chunks = tb // c
    w = HG_PAIR * HG_DK
    heads = [slice(h * HG_DK, (h + 1) * HG_DK) for h in range(HG_PAIR)]
    lbl = lbl_ref[...]
    e = jnp.exp(lbl - jnp.max(lbl, axis=0, keepdims=True))
    lb = jnp.sum(e[:layer + 1], axis=0, keepdims=True) / jnp.sum(e, axis=0, keepdims=True)
    f_half = 0.5 * (1.0 - lb)
    f_mid = lb + f_half
    ng = ng_ref[...]
    m2 = m_ref[...]

    row = lax.broadcasted_iota(jnp.int32, (c, w), 0)
    upper_row = {l: (row % (c >> l)) >= (c >> (l + 1)) for l in range(3, N_LEVELS)}
    t_i = lax.broadcasted_iota(jnp.int32, (c, HG_PAIR * c), 0)
    s_i = lax.broadcasted_iota(jnp.int32, (c, HG_PAIR * c), 1) % c
    valid = []
    for l in range(N_LEVELS):
        blk = c >> l
        valid.append((t_i // blk == s_i // blk) & (t_i % blk >= blk // 2) & (s_i % blk < blk // 2))

    def blockdiag(a, zero):
        return jnp.concatenate([jnp.concatenate([a[:, heads[0]], zero], axis=1),
                                jnp.concatenate([zero, a[:, heads[1]]], axis=1)], axis=0)

    def per_head(fn):
        return jnp.concatenate([fn(hs) for hs in heads], axis=1)

    zero_c = jnp.zeros((c, HG_DK), BF16)
    zero_s = jnp.zeros((HG_DV, HG_DK), BF16)
    nt_dims = (((1,), (1,)), ((), ()))
    tn_dims = (((0,), (0,)), ((), ()))

    def level_operand(l, qf, kk, f, b_cum, ex):
        blk = c >> l
        half = blk // 2
        if blk >= 2 * SUBLANES:
            pieces, expo = [], []
            for s0 in range(0, c, blk):
                ref = jnp.broadcast_to(b_cum[s0 + half - 1:s0 + half, :], (half, w))
                lo = slice(s0, s0 + half)
                hi = slice(s0 + half, s0 + blk)
                pieces += [kk[lo, :], qf[hi, :]]
                expo += [ref - b_cum[lo, :], b_cum[hi, :] - ref]
            y = jnp.concatenate(pieces, axis=0) * jnp.exp2(jnp.concatenate(expo, axis=0))
        elif l in HG_MXU_LEVELS:
            i = HG_MXU_LEVELS.index(l)
            y = jnp.where(upper_row[l], qf, kk) * jnp.exp2(ex[(1 + i) * c:(2 + i) * c])
        else:
            y = jnp.where(upper_row[l], qf * f, kk)
        return y.astype(BF16)

    st = [st_ref[h] for h in range(HG_PAIR)]
    n_pairs = n_chunks // 2
    fronts, raws, updates, mids, st_ins = {}, {}, {}, {}, {}

    def stage_gates(p):
        out = []
        for ci in (2 * p, 2 * p + 1):
            rs = slice(ci * c, (ci + 1) * c)
            qf = _silu(q_ref[rs, :])
            f = jnp.maximum(f_mid + f_half * jnp.tanh(0.5 * f_ref[rs, :]), lb)
            log_f = jnp.log2(f)
            p0 = log_f.astype(BF16)
            p1 = (log_f - p0.astype(F32)).astype(BF16)
            ex = jnp.dot(m2, jnp.concatenate([p0, p1], axis=0), preferred_element_type=F32)
            out.append((rs, qf, 1.0 - f, f, ex[:c], ex))
        fronts[p] = out

    def stage_levels(p):
        raws[p] = []
        for l in range(N_LEVELS):
            ys = [level_operand(l, qf, kk, f, b_cum, ex) for (_, qf, kk, f, b_cum, ex) in fronts[p]]
            raws[p].append(lax.dot_general(jnp.concatenate(ys, axis=0),
                                           jnp.concatenate([blockdiag(y, zero_c) for y in ys], axis=0),
                                           nt_dims, preferred_element_type=F32))
        updates[p] = []
        for rs, qf, kk, f, b_cum, ex in fronts[p]:
            ke = (kk * jnp.exp2(b_cum[c - 1:c, :] - b_cum)).astype(BF16)
            updates[p].append(lax.dot_general(v_ref[rs, :].astype(BF16), ke, tn_dims,
                                              preferred_element_type=F32))

    def stage_intra(p):
        mids[p] = []
        for j, (rs, qf, kk, f, b_cum, ex) in enumerate(fronts[p]):
            scores = 0.0
            for l in range(N_LEVELS):
                s_l = raws[p][l][j * c:(j + 1) * c, j * HG_PAIR * c:(j + 1) * HG_PAIR * c]
                scores = jnp.where(valid[l], s_l, scores)
            v = v_ref[rs, :]
            o = jnp.dot(scores.astype(BF16), blockdiag(v.astype(BF16), zero_c),
                        preferred_element_type=F32)
            qk = qf * kk
            o = o + per_head(lambda hs: jnp.broadcast_to(
                jnp.sum(qk[:, hs], axis=-1, keepdims=True), (c, HG_DK))) * v
            qe = (qf * jnp.exp2(b_cum)).astype(BF16)
            mids[p].append((rs, o, qe, jnp.exp2(b_cum[c - 1:c, :])))

    def stage_state(p):
        st_ins[p] = []
        for (rs, o, qe, dec), ut in zip(mids[p], updates[p]):
            s0, s1 = (jnp.transpose(s).astype(BF16) for s in st)
            st_ins[p].append(jnp.concatenate([jnp.concatenate([s0, zero_s], axis=1),
                                              jnp.concatenate([zero_s, s1], axis=1)], axis=0))
            for h, hs in enumerate(heads):
                st[h] = st[h] * dec[:, hs] + ut[hs, hs]

    def stage_out(p):
        for (rs, o, qe, dec), st_bd in zip(mids[p], st_ins[p]):
            o = o + jnp.dot(qe, st_bd, preferred_element_type=F32)
            o2 = o * o
            inv = per_head(lambda hs: jnp.broadcast_to(
                lax.rsqrt(jnp.mean(o2[:, hs], axis=-1, keepdims=True) + EPS), (c, HG_DV)))
            o_ref[rs, :] = (o * inv * ng * _silu(g_ref[rs, :])).astype(o_ref.dtype)

    stages = (stage_gates, stage_levels, stage_intra, stage_state, stage_out)
    for tick in range(n_pairs + len(stages) - 1):
        for k, stage in enumerate(stages):
            if 0 <= tick - k < n_pairs:
                stage(tick - k)
    for h in range(HG_PAIR):
        st_ref[h] = st[h]


def _hgrn2(proj, lb_logits, norm_g, layer, col0, width, tb=2048):
    _, s, bw = proj.shape
    n_heads = width // HG_DK
    assert bw == HG_PAIR * HG_DK
    assert n_heads % HG_PAIR == 0 and s % tb == 0 and tb % CHUNK == 0 and col0 % bw == 0
    sec = lambda k: (lambda hg, t: ((col0 + k * width) // bw + hg, t, 0))
    depth1 = lb_logits.shape[0]
    m2 = jnp.asarray(_decay_sum_matrix(), BF16)
    blk = 4 * tb * bw * 4 + tb * bw * 2 + m2.size * 2
    return pl.pallas_call(
        functools.partial(_hgrn_kernel, tb=tb, layer=layer),
        grid=(n_heads // HG_PAIR, s // tb),
        in_specs=[pl.BlockSpec((None, tb, bw), sec(0)),
                  pl.BlockSpec((None, tb, bw), sec(1)),
                  pl.BlockSpec((None, tb, bw), sec(2)),
                  pl.BlockSpec((None, tb, bw), sec(3)),
                  pl.BlockSpec((depth1, bw), lambda hg, t: (0, hg)),
                  pl.BlockSpec((1, bw), lambda hg, t: (0, hg)),
                  pl.BlockSpec(m2.shape, lambda hg, t: (0, 0))],
        out_specs=pl.BlockSpec((tb, bw), lambda hg, t: (t, hg)),
        out_shape=jax.ShapeDtypeStruct((s, width), BF16),
        scratch_shapes=[pltpu.VMEM((HG_PAIR, HG_DV, HG_DK), F32)],
        compiler_params=pltpu.CompilerParams(
            dimension_semantics=("parallel", "arbitrary"),
            vmem_limit_bytes=_vmem_limit(blk + 8 * 1024 * 1024)),
        name="hgrn2",
    )(proj, proj, proj, proj, lb_logits, norm_g.reshape(1, width), m2)


def kernel(x, attn_norm_g, w_in, conv_w, conv_b, conv_ln_g, conv_ln_b, hg_lb_logits, hg_norm_g,
           w_out, ffn_norm_g, w_gate, w_up, w_down, final_norm_g):
    bsz, seq, d_model = x.shape
    assert bsz == 1
    depth = w_in.shape[0]
    conv_width = conv_w.shape[2]
    hg_width = hg_norm_g.shape[1]
    d_ff = w_gate.shape[2]
    bf = 256
    assert d_ff % bf == 0

    xs = x.reshape(seq, d_model)
    for l in range(depth):
        h = _rmsnorm(xs, attn_norm_g[l], BF16)
        proj, w_out_b, w_down_b = _matmul(h, w_in[l], None, F32, bm=2048, bn=512, lhs_bufs=1,
                                          side_casts=(w_out[l], w_down[l]), side_rows=128,
                                          out_col_tile=HG_PAIR * HG_DK, row_groups=4)
        a_out = _conformer_conv(proj, conv_w[l], conv_b[l], conv_ln_g[l], conv_ln_b[l], conv_width)
        b_out = _hgrn2(proj, hg_lb_logits, hg_norm_g[l], l, 2 * conv_width, hg_width)
        xs, h, ssq = _outproj_norm(a_out, b_out, w_out_b, xs, ffn_norm_g[l], bm=1024, bn=1024)
        u = _swiglu(h, ssq, w_gate[l], w_up[l], bm=4096, bf=bf)
        xs = _matmul(u, w_down_b, xs, F32, bm=512, bn=512)
    out = _rmsnorm(xs, final_norm_g, F32)
    return out.reshape(bsz, seq, d_model)
```
